```python
import math
import jax, jax.numpy as jnp
from jax import lax
import numpy as np

D_MODEL = 4096
BATCH = 4
SEQ = 4096
DEPTH = 4

D_MIX = D_MODEL
D_HALF = D_MIX // 2
A_HEAD_DIM = 64
A_HEADS = D_HALF // (2 * A_HEAD_DIM)
Q_BLOCK = 128
S5_GROUP = 16
S5_GROUPS = D_HALF // S5_GROUP
S5_STATE = 64
CONV_WIDTH = 31
H_EXPAND = 128
H_HEADS = D_HALF // H_EXPAND
H_VDIM = D_HALF // H_HEADS
H_CHUNK = 64
D_FF = (((8 * D_MODEL + 2) // 3 + 255) // 256) * 256
EVEN_IN = 4 * D_HALF
ODD_IN = 6 * D_HALF
N_EVEN = (DEPTH + 1) // 2
N_ODD = DEPTH // 2
RMS_EPS = 1e-6
LN_EPS = 1e-5

kernel_name = 'hybrid_diffattn_s5_conformer_hgrn2_trunk'

F32 = jnp.float32


def rms_norm(x, g):
    xf = x.astype(F32)
    y = xf * lax.rsqrt(jnp.mean(xf * xf, axis=-1, keepdims=True) + RMS_EPS)
    return (y * g.astype(F32)).astype(x.dtype)


def layer_norm(x, g, b):
    xf = x.astype(F32)
    mu = jnp.mean(xf, axis=-1, keepdims=True)
    var = jnp.mean(jnp.square(xf - mu), axis=-1, keepdims=True)
    y = (xf - mu) * lax.rsqrt(var + LN_EPS) * g.astype(F32) + b.astype(F32)
    return y.astype(x.dtype)


def diff_attention(q, k, v, lam, subln_g, lambda_init):
    Bsz, S = q.shape[:2]
    nb = S // Q_BLOCK
    scale = A_HEAD_DIM ** -0.5
    kpos = jnp.arange(S)
    qb = jnp.moveaxis(q.reshape(Bsz, nb, Q_BLOCK, A_HEADS, 2, A_HEAD_DIM), 1, 0)

    def block(args):
        i, qi = args
        s = jnp.einsum('bqhcd,bkhcd->bhcqk', qi, k).astype(F32) * scale
        qpos = i * Q_BLOCK + jnp.arange(Q_BLOCK)
        causal = kpos[None, :] <= qpos[:, None]
        p = jax.nn.softmax(jnp.where(causal, s, -jnp.inf), axis=-1)
        w = p[:, :, 0] - lam * p[:, :, 1]
        return jnp.einsum('bhqk,bkhe->bqhe', w.astype(v.dtype), v)

    o = lax.map(block, (jnp.arange(nb), qb))
    o = jnp.moveaxis(o, 0, 1).reshape(Bsz, S, A_HEADS, 2 * A_HEAD_DIM)
    o = rms_norm(o, subln_g) * (1.0 - lambda_init)
    return o.reshape(Bsz, S, A_HEADS * 2 * A_HEAD_DIM)


def s5_mixer(u, lam_re, lam_im, log_step, B_re, B_im, C_re, C_im, D, w_glu):
    Bsz, S = u.shape[:2]
    uf = u.astype(F32)
    ug = uf.reshape(Bsz, S, S5_GROUPS, S5_GROUP)
    dt = jnp.exp(log_step.astype(F32))[:, None]
    lr, li = lam_re.astype(F32), lam_im.astype(F32)
    mag = jnp.exp(lr * dt)
    ang = li * dt
    ab_re, ab_im = mag * jnp.cos(ang), mag * jnp.sin(ang)
    den = lr * lr + li * li
    num_re, num_im = ab_re - 1.0, ab_im
    fz_re = (num_re * lr + num_im * li) / den
    fz_im = (num_im * lr - num_re * li) / den
    Br, Bi = B_re.astype(F32), B_im.astype(F32)
    Bb_re = fz_re[..., None] * Br - fz_im[..., None] * Bi
    Bb_im = fz_re[..., None] * Bi + fz_im[..., None] * Br
    bu_re = jnp.einsum('bsgc,gnc->bsgn', ug, Bb_re)
    bu_im = jnp.einsum('bsgc,gnc->bsgn', ug, Bb_im)
    a_re = jnp.broadcast_to(ab_re, (1, S, S5_GROUPS, S5_STATE))
    a_im = jnp.broadcast_to(ab_im, (1, S, S5_GROUPS, S5_STATE))

    def combine(e1, e2):
        a1r, a1i, b1r, b1i = e1
        a2r, a2i, b2r, b2i = e2
        return (a2r * a1r - a2i * a1i,
                a2r * a1i + a2i * a1r,
                a2r * b1r - a2i * b1i + b2r,
                a2r * b1i + a2i * b1r + b2i)

    _, _, xr, xi = lax.associative_scan(combine, (a_re, a_im, bu_re, bu_im), axis=1)
    y = (jnp.einsum('bsgn,gcn->bsgc', xr, C_re.astype(F32))
         - jnp.einsum('bsgn,gcn->bsgc', xi, C_im.astype(F32)))
    y = y.reshape(Bsz, S, D_HALF) + D.astype(F32) * uf
    z = jax.nn.gelu(y)
    return (z * jax.nn.sigmoid(z @ w_glu.astype(F32))).astype(u.dtype)


def conv_module(a, gate, w, b, ln_g, ln_b):
    h = a * jax.nn.sigmoid(gate)
    h = lax.conv_general_dilated(
        h, w[:, None, :].astype(h.dtype), window_strides=(1,),
        padding=[(CONV_WIDTH - 1, 0)],
        dimension_numbers=('NWC', 'WIO', 'NWC'),
        feature_group_count=D_HALF) + b.astype(h.dtype)
    h = layer_norm(h, ln_g, ln_b)
    return jax.nn.silu(h)


def hgrn2_mixer(q, f, i, g, lb, norm_g):
    Bsz, S = q.shape[:2]
    nc = S // H_CHUNK
    lbf = lb.astype(F32)
    qf = jax.nn.silu(q.astype(F32))
    fg = lbf + (1.0 - lbf) * jax.nn.sigmoid(f.astype(F32))
    log_f = jnp.log(fg)
    kf = 1.0 - fg
    vf = i.astype(F32)

    def chunks(t, dim):
        return t.reshape(Bsz, nc, H_CHUNK, H_HEADS, dim).transpose(0, 3, 1, 2, 4)

    qc, kc, lfc = chunks(qf, H_EXPAND), chunks(kf, H_EXPAND), chunks(log_f, H_EXPAND)
    vc = chunks(vf, H_VDIM)
    bcum = jnp.cumsum(lfc, axis=3)
    b_last = bcum[:, :, :, -1]
    b_mid = bcum[:, :, :, H_CHUNK // 2 - 1:H_CHUNK // 2]
    q_in = qc * jnp.exp(bcum - b_mid)
    k_in = kc * jnp.exp(b_mid - bcum)
    scores = jnp.einsum('bhnld,bhnmd->bhnlm', q_in, k_in)
    causal = jnp.tril(jnp.ones((H_CHUNK, H_CHUNK), dtype=bool))
    scores = jnp.where(causal, scores, 0.0)
    o_intra = jnp.einsum('bhnlm,bhnme->bhnle', scores, vc)
    k_dec = kc * jnp.exp(b_last[:, :, :, None] - bcum)
    chunk_kv = jnp.einsum('bhnld,bhnle->bhnde', k_dec, vc)
    decay = jnp.exp(b_last)

    def step(state, inp):
        d, kv = inp
        return d[..., None] * state + kv, state

    s0 = jnp.zeros((Bsz, H_HEADS, H_EXPAND, H_VDIM), F32)
    _, s_start = lax.scan(step, s0, (jnp.moveaxis(decay, 2, 0), jnp.moveaxis(chunk_kv, 2, 0)))
    s_start = jnp.moveaxis(s_start, 0, 2)
    o_inter = jnp.einsum('bhnld,bhnde->bhnle', qc * jnp.exp(bcum), s_start)
    o = (o_intra + o_inter).transpose(0, 2, 3, 1, 4).reshape(Bsz, S, H_HEADS, H_VDIM)
    o = rms_norm(o, norm_g) * jax.nn.silu(g.astype(F32)).reshape(Bsz, S, H_HEADS, H_VDIM)
    return o.reshape(Bsz, S, D_HALF).astype(q.dtype)


def swiglu(h, w_gate, w_up, w_down):
    return (jax.nn.silu(h @ w_gate) * (h @ w_up)) @ w_down


def setup_inputs(seed: int = 0) -> dict:
    key = jax.random.key(seed)
    ks = jax.random.split(key, 26)

    def nrm(k, shape, scale):
        return jax.random.normal(k, shape, F32) * scale

    x = nrm(ks[0], (BATCH, SEQ, D_MODEL), 1.0)
    norm_gains = 1.0 + nrm(ks[1], (DEPTH, 4, D_MODEL), 0.05)
    ffn_w_gate = nrm(ks[2], (DEPTH, D_MODEL, D_FF), D_MODEL ** -0.5)
    ffn_w_up = nrm(ks[3], (DEPTH, D_MODEL, D_FF), D_MODEL ** -0.5)
    ffn_w_down = nrm(ks[4], (DEPTH, D_FF, D_MODEL), D_FF ** -0.5)
    ev_w_in = nrm(ks[5], (N_EVEN, D_MODEL, EVEN_IN), D_MODEL ** -0.5)
    ev_w_out = nrm(ks[6], (N_EVEN, D_MIX, D_MODEL), D_MIX ** -0.5)
    diff_lambda = nrm(ks[7], (N_EVEN, 4, A_HEAD_DIM), 0.1)
    diff_subln = 1.0 + nrm(ks[8], (N_EVEN, 2 * A_HEAD_DIM), 0.05)
    s5_lambda_re = -0.5 + nrm(ks[9], (N_EVEN, S5_GROUPS, S5_STATE), 0.01)
    s5_lambda_im = jnp.pi * jnp.arange(S5_STATE, dtype=F32) + nrm(ks[10], (N_EVEN, S5_GROUPS, S5_STATE), 0.01)
    s5_log_step = jax.random.uniform(ks[11], (N_EVEN, S5_GROUPS), F32, math.log(1e-3), math.log(1e-1))
    s5_B_re = nrm(ks[12], (N_EVEN, S5_GROUPS, S5_STATE, S5_GROUP), (2 * S5_GROUP) ** -0.5)
    s5_B_im = nrm(ks[13], (N_EVEN, S5_GROUPS, S5_STATE, S5_GROUP), (2 * S5_GROUP) ** -0.5)
    s5_C_re = nrm(ks[14], (N_EVEN, S5_GROUPS, S5_GROUP, S5_STATE), 0.5)
    s5_C_im = nrm(ks[15], (N_EVEN, S5_GROUPS, S5_GROUP, S5_STATE), 0.5)
    s5_D = nrm(ks[16], (N_EVEN, D_HALF), 1.0)
    s5_w_glu = nrm(ks[17], (N_EVEN, D_HALF, D_HALF), D_HALF ** -0.5)
    od_w_in = nrm(ks[18], (N_ODD, D_MODEL, ODD_IN), D_MODEL ** -0.5)
    od_w_out = nrm(ks[19], (N_ODD, D_MIX, D_MODEL), D_MIX ** -0.5)
    conv_w = nrm(ks[20], (N_ODD, CONV_WIDTH, D_HALF), CONV_WIDTH ** -0.5)
    conv_b = nrm(ks[21], (N_ODD, D_HALF), 0.01)
    conv_ln_g = 1.0 + nrm(ks[22], (N_ODD, D_HALF), 0.05)
    conv_ln_b = nrm(ks[23], (N_ODD, D_HALF), 0.01)
    hgrn_lb = nrm(ks[24], (DEPTH, D_HALF), 0.1)
    hgrn_norm_g = 1.0 + nrm(ks[25], (N_ODD, H_VDIM), 0.05)
    return {'x': x, 'norm_gains': norm_gains,
            'ffn_w_gate': ffn_w_gate, 'ffn_w_up': ffn_w_up, 'ffn_w_down': ffn_w_down,
            'ev_w_in': ev_w_in, 'ev_w_out': ev_w_out,
            'diff_lambda': diff_lambda, 'diff_subln': diff_subln,
            's5_lambda_re': s5_lambda_re, 's5_lambda_im': s5_lambda_im, 's5_log_step': s5_log_step,
            's5_B_re': s5_B_re, 's5_B_im': s5_B_im, 's5_C_re': s5_C_re, 's5_C_im': s5_C_im,
            's5_D': s5_D, 's5_w_glu': s5_w_glu,
            'od_w_in': od_w_in, 'od_w_out': od_w_out,
            'conv_w': conv_w, 'conv_b': conv_b, 'conv_ln_g': conv_ln_g, 'conv_ln_b': conv_ln_b,
            'hgrn_lb': hgrn_lb, 'hgrn_norm_g': hgrn_norm_g}


def reference(x, norm_gains, ffn_w_gate, ffn_w_up, ffn_w_down, ev_w_in, ev_w_out,
              diff_lambda, diff_subln, s5_lambda_re, s5_lambda_im, s5_log_step,
              s5_B_re, s5_B_im, s5_C_re, s5_C_im, s5_D, s5_w_glu,
              od_w_in, od_w_out, conv_w, conv_b, conv_ln_g, conv_ln_b,
              hgrn_lb, hgrn_norm_g):
    Bsz, S = x.shape[:2]
    lb_all = jnp.cumsum(jax.nn.softmax(hgrn_lb.astype(F32), axis=0), axis=0)
    lb_all = lb_all - lb_all[0]
    for l in range(DEPTH):
        gains = norm_gains[l]
        h = rms_norm(x, gains[0])
        if l % 2 == 0:
            e = l // 2
            lambda_init = 0.8 - 0.6 * math.exp(-0.3 * l)
            z = h @ ev_w_in[e]
            q, k, v, u = jnp.split(z, 4, axis=-1)
            q = q.reshape(Bsz, S, A_HEADS, 2, A_HEAD_DIM)
            k = k.reshape(Bsz, S, A_HEADS, 2, A_HEAD_DIM)
            v = v.reshape(Bsz, S, A_HEADS, 2 * A_HEAD_DIM)
            lf = diff_lambda[e].astype(F32)
            lam = jnp.exp(jnp.sum(lf[0] * lf[1])) - jnp.exp(jnp.sum(lf[2] * lf[3])) + lambda_init
            o_a = diff_attention(q, k, v, lam, diff_subln[e], lambda_init)
            o_b = s5_mixer(u, s5_lambda_re[e], s5_lambda_im[e], s5_log_step[e], s5_B_re[e], s5_B_im[e],
                           s5_C_re[e], s5_C_im[e], s5_D[e], s5_w_glu[e])
            mix = jnp.concatenate([o_a, o_b.astype(o_a.dtype)], axis=-1) @ ev_w_out[e]
        else:
            o = l // 2
            z = h @ od_w_in[o]
            ca, cg, hq, hf, hi, hg = jnp.split(z, 6, axis=-1)
            o_c = conv_module(ca, cg, conv_w[o], conv_b[o], conv_ln_g[o], conv_ln_b[o])
            o_d = hgrn2_mixer(hq, hf, hi, hg, lb_all[l], hgrn_norm_g[o])
            mix = jnp.concatenate([o_c, o_d.astype(o_c.dtype)], axis=-1) @ od_w_out[o]
        x = x + rms_norm(mix, gains[1]).astype(x.dtype)
        h = rms_norm(x, gains[2])
        x = x + rms_norm(swiglu(h, ffn_w_gate[l], ffn_w_up[l], ffn_w_down[l]), gains[3]).astype(x.dtype)
    return x
```

```python
import functools
import math

import jax
import jax.numpy as jnp
from jax import lax
from jax.experimental import pallas as pl
from jax.experimental.pallas import tpu as pltpu

F32 = jnp.float32
BF16 = jnp.bfloat16

A_HEAD_DIM = 64
S5_GROUP = 16
S5_STATE = 64
S5_CHUNK = 16
CONV_WIDTH = 31
CONV_HALO = 32
H_EXPAND = 128
H_CHUNK = 64
RMS_EPS = 1e-6
LN_EPS = 1e-5
LANES = 128
VMEM_LIMIT_BYTES = 56 * 1024 * 1024
NEG_BIG = -1e30


def _params(*sem):
    return pltpu.CompilerParams(dimension_semantics=sem, vmem_limit_bytes=VMEM_LIMIT_BYTES)


def _sigmoid(x):
    return 1.0 / (1.0 + jnp.exp(-x))


def _dot(a, b):
    return jnp.dot(a, b, preferred_element_type=F32)


def _dot_nt(a, b):
    return lax.dot_general(a, b, (((1,), (1,)), ((), ())), preferred_element_type=F32)


def _dot_tn(a, b):
    return lax.dot_general(a, b, (((0,), (0,)), ((), ())), preferred_element_type=F32)


def _rms(x, g):
    return x * lax.rsqrt(jnp.mean(x * x, axis=-1, keepdims=True) + RMS_EPS) * g


def _rmsnorm_kernel(x_ref, g_ref, o_ref):
    o_ref[...] = _rms(x_ref[...].astype(F32), g_ref[...]).astype(o_ref.dtype)


def rmsnorm(x, g, *, rows=512):
    t, d = x.shape
    return pl.pallas_call(
        _rmsnorm_kernel,
        out_shape=jax.ShapeDtypeStruct((t, d), BF16),
        grid=(t // rows,),
        in_specs=[pl.BlockSpec((rows, d), lambda i: (i, 0)), pl.BlockSpec((1, d), lambda i: (0, 0))],
        out_specs=pl.BlockSpec((rows, d), lambda i: (i, 0)),
        compiler_params=_params("parallel"),
        name="rmsnorm",
    )(x, g.reshape(1, d).astype(F32))


def _resnorm_kernel(y_ref, x_ref, gpost_ref, gpre_ref, xo_ref, ho_ref):
    xn = x_ref[...] + _rms(y_ref[...].astype(F32), gpost_ref[...])
    xo_ref[...] = xn
    ho_ref[...] = _rms(xn, gpre_ref[...]).astype(ho_ref.dtype)


def _resnorm_last_kernel(y_ref, x_ref, gpost_ref, xo_ref):
    xo_ref[...] = x_ref[...] + _rms(y_ref[...].astype(F32), gpost_ref[...])


def resnorm(y, x, g_post, g_pre, *, rows=256):
    t, d = x.shape
    row_spec = pl.BlockSpec((rows, d), lambda i: (i, 0))
    gain_spec = pl.BlockSpec((1, d), lambda i: (0, 0))
    if g_pre is None:
        return pl.pallas_call(
            _resnorm_last_kernel,
            out_shape=jax.ShapeDtypeStruct((t, d), F32),
            grid=(t // rows,),
            in_specs=[row_spec, row_spec, gain_spec],
            out_specs=row_spec,
            compiler_params=_params("parallel"),
            name="resnorm_last",
        )(y, x, g_post.reshape(1, d).astype(F32)), None
    return pl.pallas_call(
        _resnorm_kernel,
        out_shape=(jax.ShapeDtypeStruct((t, d), F32), jax.ShapeDtypeStruct((t, d), BF16)),
        grid=(t // rows,),
        in_specs=[row_spec, row_spec, gain_spec, gain_spec],
        out_specs=(row_spec, row_spec),
        compiler_params=_params("parallel"),
        name="resnorm",
    )(y, x, g_post.reshape(1, d).astype(F32), g_pre.reshape(1, d).astype(F32))


def _mm_kernel(x_ref, w_ref, o_ref):
    o_ref[...] = _dot(x_ref[...], w_ref[...]).astype(o_ref.dtype)


def matmul(x, w, *, tm, tn, out_dtype=BF16):
    m, k = x.shape
    n = w.shape[1]
    return pl.pallas_call(
        _mm_kernel,
        out_shape=jax.ShapeDtypeStruct((m, n), out_dtype),
        grid=(m // tm, n // tn),
        in_specs=[pl.BlockSpec((tm, k), lambda i, j: (i, 0)), pl.BlockSpec((k, tn), lambda i, j: (0, j))],
        out_specs=pl.BlockSpec((tm, tn), lambda i, j: (i, j)),
        compiler_params=_params("parallel", "arbitrary"),
        name="matmul",
    )(x, w)


def _mm2_kernel(x1_ref, x2_ref, w1_ref, w2_ref, o_ref):
    o_ref[...] = (_dot(x1_ref[...], w1_ref[...]) + _dot(x2_ref[...], w2_ref[...])).astype(o_ref.dtype)


def matmul_cat(x1, x2, w, *, tm, tn, out_dtype=BF16):
    m, kh = x1.shape
    n = w.shape[1]
    return pl.pallas_call(
        _mm2_kernel,
        out_shape=jax.ShapeDtypeStruct((m, n), out_dtype),
        grid=(m // tm, n // tn),
        in_specs=[pl.BlockSpec((tm, kh), lambda i, j: (i, 0)), pl.BlockSpec((tm, kh), lambda i, j: (i, 0)),
                  pl.BlockSpec((kh, tn), lambda i, j: (0, j)), pl.BlockSpec((kh, tn), lambda i, j: (1, j))],
        out_specs=pl.BlockSpec((tm, tn), lambda i, j: (i, j)),
        compiler_params=_params("parallel", "arbitrary"),
        name="matmul_cat",
    )(x1, x2, w, w)


def _gateup_kernel(x_ref, wg_ref, wu_ref, o_ref):
    x = x_ref[...]
    g = _dot(x, wg_ref[...])
    u = _dot(x, wu_ref[...])
    o_ref[...] = (g * _sigmoid(g) * u).astype(o_ref.dtype)


def swiglu_gate_up(x, wg, wu, *, tm, tn):
    m, k = x.shape
    n = wg.shape[1]
    w_spec = pl.BlockSpec((k, tn), lambda i, j: (0, j))
    return pl.pallas_call(
        _gateup_kernel,
        out_shape=jax.ShapeDtypeStruct((m, n), BF16),
        grid=(m // tm, n // tn),
        in_specs=[pl.BlockSpec((tm, k), lambda i, j: (i, 0)), w_spec, w_spec],
        out_specs=pl.BlockSpec((tm, tn), lambda i, j: (i, j)),
        compiler_params=_params("parallel", "arbitrary"),
        name="swiglu_gate_up",
    )(x, wg, wu)


def _diffattn_kernel(lam_ref, q_ref, k_ref, v_ref, g_ref, o_ref, m_s, l_s, acc_s, *, blk, lam_init):
    i = pl.program_id(2)
    q = q_ref[...]
    lane = lax.broadcasted_iota(jnp.int32, q.shape, 1)
    qs = q * (A_HEAD_DIM ** -0.5)
    zero = jnp.zeros_like(qs)
    q_halves = (jnp.where(lane < A_HEAD_DIM, qs, zero), jnp.where(lane >= A_HEAD_DIM, qs, zero))

    m_s[...] = jnp.full(m_s.shape, NEG_BIG, F32)
    l_s[...] = jnp.zeros(l_s.shape, F32)
    acc_s[...] = jnp.zeros(acc_s.shape, F32)

    def step(j, diagonal):
        start = pl.multiple_of(j * blk, blk)
        kj = k_ref[pl.ds(start, blk), :]
        vj = v_ref[pl.ds(start, blk), :]
        for c in range(2):
            s = _dot_nt(q_halves[c], kj)
            if diagonal:
                row = lax.broadcasted_iota(jnp.int32, s.shape, 0)
                col = lax.broadcasted_iota(jnp.int32, s.shape, 1)
                s = jnp.where(col <= row, s, NEG_BIG)
            m_prev = m_s[c]
            m_new = jnp.maximum(m_prev, jnp.max(s, axis=-1, keepdims=True))
            alpha = jnp.exp(m_prev - m_new)
            p = jnp.exp(s - m_new)
            l_s[c] = alpha * l_s[c] + jnp.sum(p, axis=-1, keepdims=True)
            acc_s[c] = alpha * acc_s[c] + _dot(p.astype(BF16), vj)
            m_s[c] = m_new

    def body(j, carry):
        step(j, False)
        return carry

    lax.fori_loop(0, i, body, 0)
    step(i, True)

    o = acc_s[0] / l_s[0] - lam_ref[0] * (acc_s[1] / l_s[1])
    o_ref[...] = (_rms(o, g_ref[...]) * (1.0 - lam_init)).astype(o_ref.dtype)


def diff_attention(z3, lam, subln_g, lam_init, *, blk):
    b, s, n = z3.shape
    heads = n // 4 // LANES
    kern = functools.partial(_diffattn_kernel, blk=blk, lam_init=lam_init)
    return pl.pallas_call(
        kern,
        out_shape=jax.ShapeDtypeStruct((b, s, heads * LANES), BF16),
        grid=(b, heads, s // blk),
        in_specs=[
            pl.BlockSpec(memory_space=pltpu.SMEM),
            pl.BlockSpec((None, blk, LANES), lambda bi, h, i: (bi, i, h)),
            pl.BlockSpec((None, s, LANES), lambda bi, h, i: (bi, 0, heads + h)),
            pl.BlockSpec((None, s, LANES), lambda bi, h, i: (bi, 0, 2 * heads + h)),
            pl.BlockSpec((1, LANES), lambda bi, h, i: (0, 0)),
        ],
        out_specs=pl.BlockSpec((None, blk, LANES), lambda bi, h, i: (bi, i, h)),
        scratch_shapes=[pltpu.VMEM((2, blk, 1), F32), pltpu.VMEM((2, blk, 1), F32),
                        pltpu.VMEM((2, blk, LANES), F32)],
        compiler_params=_params("parallel", "parallel", "arbitrary"),
        name="diff_attention",
    )(lam.reshape(1).astype(F32), z3, z3, z3, subln_g.reshape(1, LANES).astype(F32))


def _s5_tables(lam_re, lam_im, log_step, b_re, b_im, c_re, c_im, n_chunks):
    hp = lax.Precision.HIGHEST
    g_cnt, n = lam_re.shape
    ch = b_re.shape[-1]
    big_l = S5_CHUNK
    dt = jnp.exp(log_step.astype(F32))[:, None]
    lr, li = lam_re.astype(F32), lam_im.astype(F32)
    mag = jnp.exp(lr * dt)
    ang = li * dt
    a_re, a_im = mag * jnp.cos(ang), mag * jnp.sin(ang)
    den = lr * lr + li * li
    num_re, num_im = a_re - 1.0, a_im
    fz_re = (num_re * lr + num_im * li) / den
    fz_im = (num_im * lr - num_re * li) / den
    br, bi = b_re.astype(F32), b_im.astype(F32)
    bb_re = fz_re[..., None] * br - fz_im[..., None] * bi
    bb_im = fz_re[..., None] * bi + fz_im[..., None] * br

    pr, pi = [jnp.ones_like(a_re)], [jnp.zeros_like(a_im)]
    for _ in range(big_l):
        pr.append(pr[-1] * a_re - pi[-1] * a_im)
        pi.append(pr[-2] * a_im + pi[-1] * a_re)
    pw_re, pw_im = jnp.stack(pr), jnp.stack(pi)

    p_re = pw_re[:big_l, :, :, None] * bb_re[None] - pw_im[:big_l, :, :, None] * bb_im[None]
    p_im = pw_re[:big_l, :, :, None] * bb_im[None] + pw_im[:big_l, :, :, None] * bb_re[None]
    cr, ci = c_re.astype(F32), c_im.astype(F32)
    k_lag = (jnp.einsum('gcn,jgnd->jgcd', cr, p_re, precision=hp)
             - jnp.einsum('gcn,jgnd->jgcd', ci, p_im, precision=hp))
    lag = jnp.arange(big_l)[None, :] - jnp.arange(big_l)[:, None]
    k_st = jnp.where((lag >= 0)[:, :, None, None, None], k_lag[jnp.clip(lag, 0, big_l - 1)], 0.0)
    toeplitz = k_st.transpose(2, 0, 4, 1, 3).reshape(g_cnt, big_l * ch, big_l * ch)

    w_in_re = p_re[::-1].transpose(1, 0, 3, 2).reshape(g_cnt, big_l * ch, n)
    w_in_im = p_im[::-1].transpose(1, 0, 3, 2).reshape(g_cnt, big_l * ch, n)
    ar, ai = pw_re[1:], pw_im[1:]
    wo_re = (cr[None] * ar[:, :, None, :] - ci[None] * ai[:, :, None, :])
    wo_im = -(cr[None] * ai[:, :, None, :] + ci[None] * ar[:, :, None, :])
    wo_re = wo_re.transpose(1, 3, 0, 2).reshape(g_cnt, n, big_l * ch)
    wo_im = wo_im.transpose(1, 3, 0, 2).reshape(g_cnt, n, big_l * ch)

    def pair_blockdiag(w):
        gp = w.reshape(g_cnt // 2, 2, *w.shape[1:])
        z = jnp.zeros_like(gp[:, 0])
        return jnp.concatenate([jnp.concatenate([gp[:, 0], z], axis=2),
                                jnp.concatenate([z, gp[:, 1]], axis=2)], axis=1)

    w_in = jnp.concatenate([pair_blockdiag(w_in_re), pair_blockdiag(w_in_im)], axis=2)
    w_out = jnp.concatenate([pair_blockdiag(wo_re), pair_blockdiag(wo_im)], axis=1)

    sr, si = [pw_re[big_l]], [pw_im[big_l]]
    steps = max(1, int(math.ceil(math.log2(n_chunks))))
    for _ in range(steps - 1):
        sr.append(sr[-1] * sr[-1] - si[-1] * si[-1])
        si.append(2.0 * sr[-2] * si[-1])
    scan_pw = jnp.stack(sr + si, axis=1).reshape(g_cnt // 2, 2, 2 * steps, n)
    scan_pw = scan_pw.transpose(0, 2, 1, 3).reshape(g_cnt // 2, 2 * steps, 2 * n)
    return toeplitz.astype(BF16), w_in.astype(BF16), w_out.astype(BF16), scan_pw, steps


def _s5_kernel(u_ref, toe_ref, win_ref, wout_ref, pw_ref, y_ref, *, n_chunks, steps):
    u = u_ref[...]
    half = u.shape[1] // 2
    inj = _dot(u, win_ref[...])
    nl = inj.shape[1] // 2
    xr, xi = inj[:, :nl], inj[:, nl:]
    chunk = lax.broadcasted_iota(jnp.int32, xr.shape, 0) % n_chunks
    for t in range(steps):
        d = 1 << t
        ar = pw_ref[t:t + 1, :]
        ai = pw_ref[steps + t:steps + t + 1, :]
        sr = pltpu.roll(xr, d, 0)
        si = pltpu.roll(xi, d, 0)
        ok = chunk >= d
        xr, xi = (xr + jnp.where(ok, ar * sr - ai * si, 0.0),
                  xi + jnp.where(ok, ar * si + ai * sr, 0.0))
    first = chunk == 0
    xsr = jnp.where(first, 0.0, pltpu.roll(xr, 1, 0))
    xsi = jnp.where(first, 0.0, pltpu.roll(xi, 1, 0))
    xs = jnp.concatenate([xsr, xsi], axis=1).astype(BF16)
    y = _dot(xs, wout_ref[...])
    y_local = jnp.concatenate([_dot(u[:, :half], toe_ref[0]), _dot(u[:, half:], toe_ref[1])], axis=1)
    y_ref[...] = (y + y_local).astype(y_ref.dtype)


def s5_scan(u3, tables):
    toeplitz, w_in, w_out, scan_pw, steps = tables
    b, s, dh = u3.shape
    big_l, ch = S5_CHUNK, S5_GROUP
    pairs = dh // ch // 2
    n_chunks = s // big_l
    width = 2 * big_l * ch
    u_cm = (u3.reshape(b, n_chunks, big_l, pairs, 2, ch).transpose(3, 0, 1, 4, 2, 5)
            .reshape(pairs, b * n_chunks, width))
    kern = functools.partial(_s5_kernel, n_chunks=n_chunks, steps=steps)
    y_cm = pl.pallas_call(
        kern,
        out_shape=jax.ShapeDtypeStruct((pairs, b * n_chunks, width), BF16),
        grid=(pairs,),
        in_specs=[
            pl.BlockSpec((None, b * n_chunks, width), lambda p: (p, 0, 0)),
            pl.BlockSpec((2, big_l * ch, big_l * ch), lambda p: (p, 0, 0)),
            pl.BlockSpec((None, width, 4 * S5_STATE), lambda p: (p, 0, 0)),
            pl.BlockSpec((None, 4 * S5_STATE, width), lambda p: (p, 0, 0)),
            pl.BlockSpec((None, 2 * steps, 2 * S5_STATE), lambda p: (p, 0, 0)),
        ],
        out_specs=pl.BlockSpec((None, b * n_chunks, width), lambda p: (p, 0, 0)),
        compiler_params=_params("parallel"),
        name="s5_scan",
    )(u_cm, toeplitz, w_in, w_out, scan_pw)
    return (y_cm.reshape(pairs, b, n_chunks, 2, big_l, ch).transpose(1, 2, 4, 0, 3, 5)
            .reshape(b, s, dh))


def _s5_glu_kernel(y_ref, u_ref, d_ref, w_ref, o_ref):
    y = y_ref[...].astype(F32) + d_ref[...] * u_ref[...].astype(F32)
    z = 0.5 * y * (1.0 + jnp.tanh(math.sqrt(2.0 / math.pi) * (y + 0.044715 * (y * y * y))))
    gate = _dot(z.astype(BF16), w_ref[...])
    o_ref[...] = (z * _sigmoid(gate)).astype(o_ref.dtype)


def s5_glu(y, z2, d_skip, w_glu, *, rows=512):
    t, dh = y.shape
    return pl.pallas_call(
        _s5_glu_kernel,
        out_shape=jax.ShapeDtypeStruct((t, dh), BF16),
        grid=(t // rows,),
        in_specs=[pl.BlockSpec((rows, dh), lambda i: (i, 0)), pl.BlockSpec((rows, dh), lambda i: (i, 3)),
                  pl.BlockSpec((1, dh), lambda i: (0, 0)), pl.BlockSpec((dh, dh), lambda i: (0, 0))],
        out_specs=pl.BlockSpec((rows, dh), lambda i: (i, 0)),
        compiler_params=_params("parallel"),
        name="s5_glu",
    )(y, z2, d_skip.reshape(1, dh).astype(F32), w_glu)


CONV_ROWS = 64
CONV_COLS = 256


def _conv_kernel(a_ref, g_ref, w_ref, b_ref, lng_ref, lnb_ref, o_ref, h_s, y_s, *, rows):
    i = pl.program_id(1)

    @pl.when(i == 0)
    def _():
        h_s[0:CONV_HALO, :] = jnp.zeros((CONV_HALO, h_s.shape[1]), F32)

    @pl.when(i > 0)
    def _():
        h_s[0:CONV_HALO, :] = h_s[rows:rows + CONV_HALO, :]

    h_s[CONV_HALO:CONV_HALO + rows, :] = a_ref[...].astype(F32) * _sigmoid(g_ref[...].astype(F32))
    first_tap = CONV_HALO - (CONV_WIDTH - 1)

    def col_block(cb, carry):
        c0 = pl.multiple_of(cb * CONV_COLS, CONV_COLS)
        cols = pl.ds(c0, CONV_COLS)
        for r0 in range(0, rows, CONV_ROWS):
            acc = jnp.zeros((CONV_ROWS, CONV_COLS), F32)
            for j in range(CONV_WIDTH):
                acc = acc + w_ref[j:j + 1, cols] * h_s[r0 + first_tap + j:r0 + first_tap + j + CONV_ROWS, cols]
            y_s[r0:r0 + CONV_ROWS, cols] = acc + b_ref[:, cols]
        return carry

    lax.fori_loop(0, h_s.shape[1] // CONV_COLS, col_block, 0)
    y = y_s[...]
    mu = jnp.mean(y, axis=-1, keepdims=True)
    yc = y - mu
    var = jnp.mean(yc * yc, axis=-1, keepdims=True)
    yn = yc * lax.rsqrt(var + LN_EPS) * lng_ref[...] + lnb_ref[...]
    o_ref[...] = (yn * _sigmoid(yn)).astype(o_ref.dtype)


def conv_module(z3, w, bias, ln_g, ln_b, *, rows=256):
    b, s, n = z3.shape
    dh = n // 6
    kern = functools.partial(_conv_kernel, rows=rows)
    vec = lambda v: v.reshape(1, dh).astype(F32)
    vec_spec = pl.BlockSpec((1, dh), lambda bi, i: (0, 0))
    return pl.pallas_call(
        kern,
        out_shape=jax.ShapeDtypeStruct((b, s, dh), BF16),
        grid=(b, s // rows),
        in_specs=[pl.BlockSpec((None, rows, dh), lambda bi, i: (bi, i, 0)),
                  pl.BlockSpec((None, rows, dh), lambda bi, i: (bi, i, 1)),
                  pl.BlockSpec((CONV_WIDTH, dh), lambda bi, i: (0, 0)), vec_spec, vec_spec, vec_spec],
        out_specs=pl.BlockSpec((None, rows, dh), lambda bi, i: (bi, i, 0)),
        scratch_shapes=[pltpu.VMEM((CONV_HALO + rows, dh), F32), pltpu.VMEM((rows, dh), F32)],
        compiler_params=_params("arbitrary", "arbitrary"),
        name="conv_module",
    )(z3, z3, w.astype(F32), vec(bias), vec(ln_g), vec(ln_b))


def _split3(x):
    hi = x.astype(BF16)
    r1 = x - hi.astype(F32)
    mid = r1.astype(BF16)
    lo = (r1 - mid.astype(F32)).astype(BF16)
    return hi, mid, lo


def _hgrn_kernel(q_ref, f_ref, i_ref, g_ref, lb_ref, ng_ref, o_ref, st_s, *, rows):
    @pl.when(pl.program_id(2) == 0)
    def _():
        st_s[...] = jnp.zeros(st_s.shape, F32)

    big_l = H_CHUNK
    lb = lb_ref[...]
    r_i = lax.broadcasted_iota(jnp.int32, (big_l, big_l), 0)
    c_i = lax.broadcasted_iota(jnp.int32, (big_l, big_l), 1)
    causal = c_i <= r_i
    tri = jnp.where(causal, 1.0, 0.0).astype(BF16)
    for c0 in range(0, rows, big_l):
        sl = slice(c0, c0 + big_l)
        q = q_ref[sl, :].astype(F32)
        q = q * _sigmoid(q)
        fg = lb + (1.0 - lb) * _sigmoid(f_ref[sl, :].astype(F32))
        log_f = jnp.log(fg)
        k = 1.0 - fg
        v = i_ref[sl, :]
        pieces = jnp.concatenate(_split3(log_f), axis=1)
        cum3 = _dot(tri, pieces)
        dk = log_f.shape[1]
        bcum = cum3[:, :dk] + cum3[:, dk:2 * dk] + cum3[:, 2 * dk:]
        b_last = bcum[big_l - 1:big_l, :]
        b_mid = bcum[big_l // 2 - 1:big_l // 2, :]
        q_in = (q * jnp.exp(bcum - b_mid)).astype(BF16)
        k_in = (k * jnp.exp(b_mid - bcum)).astype(BF16)
        scores = jnp.where(causal, _dot_nt(q_in, k_in), 0.0)
        st = st_s[...]
        q_dec = (q * jnp.exp(bcum)).astype(BF16)
        o = _dot(scores.astype(BF16), v) + _dot_nt(q_dec, st.astype(BF16))
        k_dec = (k * jnp.exp(b_last - bcum)).astype(BF16)
        st_s[...] = st * jnp.exp(b_last) + _dot_tn(v, k_dec)
        gate = g_ref[sl, :].astype(F32)
        o_ref[sl, :] = (_rms(o, ng_ref[...]) * (gate * _sigmoid(gate))).astype(o_ref.dtype)


def hgrn2(z3, lb, norm_g, *, rows=256):
    b, s, n = z3.shape
    dh = n // 6
    heads = dh // H_EXPAND
    kern = functools.partial(_hgrn_kernel, rows=rows)

    def col(group):
        return pl.BlockSpec((None, rows, LANES), lambda bi, h, i: (bi, i, group * heads + h))

    return pl.pallas_call(
        kern,
        out_shape=jax.ShapeDtypeStruct((b, s, dh), BF16),
        grid=(b, heads, s // rows),
        in_specs=[col(2), col(3), col(4), col(5),
                  pl.BlockSpec((1, LANES), lambda bi, h, i: (0, h)),
                  pl.BlockSpec((1, LANES), lambda bi, h, i: (0, 0))],
        out_specs=pl.BlockSpec((None, rows, LANES), lambda bi, h, i: (bi, i, h)),
        scratch_shapes=[pltpu.VMEM((H_EXPAND, H_EXPAND), F32)],
        compiler_params=_params("parallel", "parallel", "arbitrary"),
        name="hgrn2",
    )(z3, z3, z3, z3, lb.reshape(1, dh).astype(F32), norm_g.reshape(1, LANES).astype(F32))


def _tile(n, want):
    if n <= want:
        return n
    t = want - want % LANES
    while n % t:
        t -= LANES
    return t


def kernel(x, norm_gains, ffn_w_gate, ffn_w_up, ffn_w_down, ev_w_in, ev_w_out, diff_lambda, diff_subln, s5_lambda_re, s5_lambda_im, s5_log_step, s5_B_re, s5_B_im, s5_C_re, s5_C_im, s5_D, s5_w_glu, od_w_in, od_w_out, conv_w, conv_b, conv_ln_g, conv_ln_b, hgrn_lb, hgrn_norm_g):
    bsz, seq, d = x.shape
    depth = norm_gains.shape[0]
    t = bsz * seq
    dh = d // 2
    d_ff = ffn_w_gate.shape[-1]
    tm = _tile(t, 1024)
    attn_blk = _tile(seq, 512)

    lb_all = jnp.cumsum(jax.nn.softmax(hgrn_lb.astype(F32), axis=0), axis=0)
    lb_all = lb_all - lb_all[0]

    xf = x.reshape(t, d)
    h = rmsnorm(xf, norm_gains[0, 0])
    for l in range(depth):
        gains = norm_gains[l]
        if l % 2 == 0:
            e = l // 2
            lam_init = 0.8 - 0.6 * math.exp(-0.3 * l)
            z = matmul(h, ev_w_in[e].astype(BF16), tm=tm, tn=_tile(4 * dh, 1024))
            z3 = z.reshape(bsz, seq, 4 * dh)
            lf = diff_lambda[e].astype(F32)
            lam = jnp.exp(jnp.sum(lf[0] * lf[1])) - jnp.exp(jnp.sum(lf[2] * lf[3])) + lam_init
            o_a = diff_attention(z3, lam, diff_subln[e], lam_init, blk=attn_blk).reshape(t, dh)
            tables = _s5_tables(s5_lambda_re[e], s5_lambda_im[e], s5_log_step[e], s5_B_re[e], s5_B_im[e],
                                s5_C_re[e], s5_C_im[e], seq // S5_CHUNK)
            y = s5_scan(z3[:, :, 3 * dh:], tables).reshape(t, dh)
            o_b = s5_glu(y, z, s5_D[e], s5_w_glu[e].astype(BF16))
            mix = matmul_cat(o_a, o_b, ev_w_out[e].astype(BF16), tm=tm, tn=_tile(d, 1024))
        else:
            o = l // 2
            z = matmul(h, od_w_in[o].astype(BF16), tm=tm, tn=_tile(6 * dh, 1024))
            z3 = z.reshape(bsz, seq, 6 * dh)
            o_c = conv_module(z3, conv_w[o], conv_b[o], conv_ln_g[o], conv_ln_b[o]).reshape(t, dh)
            o_d = hgrn2(z3, lb_all[l], hgrn_norm_g[o]).reshape(t, dh)
            mix = matmul_cat(o_c, o_d, od_w_out[o].astype(BF16), tm=tm, tn=_tile(d, 1024))
        xf, h = resnorm(mix, xf, gains[1], gains[2])
        hidden = swiglu_gate_up(h, ffn_w_gate[l].astype(BF16), ffn_w_up[l].astype(BF16), tm=tm, tn=_tile(d_ff, 256))
        down = matmul(hidden, ffn_w_down[l].astype(BF16), tm=_tile(t, 512), tn=_tile(d, 512))
        xf, h = resnorm(down, xf, gains[3], norm_gains[l + 1, 0] if l + 1 < depth else None)
    return xf.reshape(bsz, seq, d)
```

```python
import functools
import math

import jax
import jax.numpy as jnp
from jax import lax
from jax.experimental import pallas as pl
from jax.experimental.pallas import tpu as pltpu

F32 = jnp.float32
BF16 = jnp.bfloat16

A_HEAD_DIM = 64
S5_GROUP = 16
S5_STATE = 64
S5_CHUNK = 16
CONV_WIDTH = 31
CONV_HALO = 32
H_EXPAND = 128
H_CHUNK = 64
RMS_EPS = 1e-6
LN_EPS = 1e-5
LANES = 128
SUBLANES = 8
VMEM_LIMIT_BYTES = 56 * 1024 * 1024
NEG_BIG = -1e30
S5_BLOCK_GROUPS = LANES // S5_GROUP
S5_BLOCK_STATE = S5_BLOCK_GROUPS * S5_STATE


def _params(*sem):
    return pltpu.CompilerParams(dimension_semantics=sem, vmem_limit_bytes=VMEM_LIMIT_BYTES)


def _sigmoid(x):
    return 1.0 / (1.0 + jnp.exp(-x))


def _dot(a, b):
    return jnp.dot(a, b, preferred_element_type=F32)


def _dot_nt(a, b):
    return lax.dot_general(a, b, (((1,), (1,)), ((), ())), preferred_element_type=F32)


def _dot_tn(a, b):
    return lax.dot_general(a, b, (((0,), (0,)), ((), ())), preferred_element_type=F32)


def _rms(x, g):
    return x * lax.rsqrt(jnp.mean(x * x, axis=-1, keepdims=True) + RMS_EPS) * g


def _rmsnorm_kernel(x_ref, g_ref, o_ref):
    o_ref[...] = _rms(x_ref[...].astype(F32), g_ref[...]).astype(o_ref.dtype)


def rmsnorm(x, g, *, rows=512):
    t, d = x.shape
    return pl.pallas_call(
        _rmsnorm_kernel,
        out_shape=jax.ShapeDtypeStruct((t, d), BF16),
        grid=(t // rows,),
        in_specs=[pl.BlockSpec((rows, d), lambda i: (i, 0)), pl.BlockSpec((1, d), lambda i: (0, 0))],
        out_specs=pl.BlockSpec((rows, d), lambda i: (i, 0)),
        compiler_params=_params("parallel"),
        name="rmsnorm",
    )(x, g.reshape(1, d).astype(F32))


def _resnorm_kernel(y_ref, x_ref, gpost_ref, gpre_ref, xo_ref, ho_ref):
    xn = x_ref[...] + _rms(y_ref[...].astype(F32), gpost_ref[...])
    xo_ref[...] = xn
    ho_ref[...] = _rms(xn, gpre_ref[...]).astype(ho_ref.dtype)


def _resnorm_last_kernel(y_ref, x_ref, gpost_ref, xo_ref):
    xo_ref[...] = x_ref[...] + _rms(y_ref[...].astype(F32), gpost_ref[...])


def resnorm(y, x, g_post, g_pre, *, rows=256):
    t, d = x.shape
    row_spec = pl.BlockSpec((rows, d), lambda i: (i, 0))
    gain_spec = pl.BlockSpec((1, d), lambda i: (0, 0))
    if g_pre is None:
        return pl.pallas_call(
            _resnorm_last_kernel,
            out_shape=jax.ShapeDtypeStruct((t, d), F32),
            grid=(t // rows,),
            in_specs=[row_spec, row_spec, gain_spec],
            out_specs=row_spec,
            compiler_params=_params("parallel"),
            name="resnorm_last",
        )(y, x, g_post.reshape(1, d).astype(F32)), None
    return pl.pallas_call(
        _resnorm_kernel,
        out_shape=(jax.ShapeDtypeStruct((t, d), F32), jax.ShapeDtypeStruct((t, d), BF16)),
        grid=(t // rows,),
        in_specs=[row_spec, row_spec, gain_spec, gain_spec],
        out_specs=(row_spec, row_spec),
        compiler_params=_params("parallel"),
        name="resnorm",
    )(y, x, g_post.reshape(1, d).astype(F32), g_pre.reshape(1, d).astype(F32))


def _mm_kernel(x_ref, w_ref, o_ref):
    o_ref[...] = _dot(x_ref[...], w_ref[...]).astype(o_ref.dtype)


def matmul(x, w, *, tm, tn, out_dtype=BF16):
    m, k = x.shape
    n = w.shape[1]
    return pl.pallas_call(
        _mm_kernel,
        out_shape=jax.ShapeDtypeStruct((m, n), out_dtype),
        grid=(m // tm, n // tn),
        in_specs=[pl.BlockSpec((tm, k), lambda i, j: (i, 0)), pl.BlockSpec((k, tn), lambda i, j: (0, j))],
        out_specs=pl.BlockSpec((tm, tn), lambda i, j: (i, j)),
        compiler_params=_params("parallel", "arbitrary"),
        name="matmul",
    )(x, w)


def _mm_chunk_major_kernel(x_ref, w_ref, o_ref):
    acc = _dot(x_ref[...], w_ref[...])
    for p in range(o_ref.shape[0]):
        o_ref[p] = acc[:, p * LANES:(p + 1) * LANES].astype(o_ref.dtype)


def matmul_chunk_major(x, w, *, tn):
    t, k = x.shape
    n = w.shape[1]
    big_l = S5_CHUNK
    r = t // big_l
    return pl.pallas_call(
        _mm_chunk_major_kernel,
        out_shape=jax.ShapeDtypeStruct((big_l, n // LANES, r, LANES), BF16),
        grid=(big_l, n // tn),
        in_specs=[pl.BlockSpec((r, k), lambda l, j: (0, l)), pl.BlockSpec((k, tn), lambda l, j: (0, j))],
        out_specs=pl.BlockSpec((None, tn // LANES, r, LANES), lambda l, j: (l, j, 0, 0)),
        compiler_params=_params("parallel", "arbitrary"),
        name="matmul_chunk_major",
    )(x.reshape(r, big_l * k), w)


def _mm2_kernel(x1_ref, x2_ref, w1_ref, w2_ref, o_ref):
    o_ref[...] = (_dot(x1_ref[...], w1_ref[...]) + _dot(x2_ref[...], w2_ref[...])).astype(o_ref.dtype)


def matmul_cat(x1, x2, w, *, tm, tn, out_dtype=BF16):
    m, kh = x1.shape
    n = w.shape[1]
    return pl.pallas_call(
        _mm2_kernel,
        out_shape=jax.ShapeDtypeStruct((m, n), out_dtype),
        grid=(m // tm, n // tn),
        in_specs=[pl.BlockSpec((tm, kh), lambda i, j: (i, 0)), pl.BlockSpec((tm, kh), lambda i, j: (i, 0)),
                  pl.BlockSpec((kh, tn), lambda i, j: (0, j)), pl.BlockSpec((kh, tn), lambda i, j: (1, j))],
        out_specs=pl.BlockSpec((tm, tn), lambda i, j: (i, j)),
        compiler_params=_params("parallel", "arbitrary"),
        name="matmul_cat",
    )(x1, x2, w, w)


def _gateup_kernel(x_ref, wg_ref, wu_ref, o_ref):
    x = x_ref[...]
    g = _dot(x, wg_ref[...])
    u = _dot(x, wu_ref[...])
    o_ref[...] = (g * _sigmoid(g) * u).astype(o_ref.dtype)


def swiglu_gate_up(x, wg, wu, *, tm, tn):
    m, k = x.shape
    n = wg.shape[1]
    w_spec = pl.BlockSpec((k, tn), lambda i, j: (0, j))
    return pl.pallas_call(
        _gateup_kernel,
        out_shape=jax.ShapeDtypeStruct((m, n), BF16),
        grid=(m // tm, n // tn),
        in_specs=[pl.BlockSpec((tm, k), lambda i, j: (i, 0)), w_spec, w_spec],
        out_specs=pl.BlockSpec((tm, tn), lambda i, j: (i, j)),
        compiler_params=_params("parallel", "arbitrary"),
        name="swiglu_gate_up",
    )(x, wg, wu)


def _diffattn_kernel(lam_ref, q_ref, k_ref, v_ref, g_ref, o_ref, q2_s, vx_s, m_s, acc_s, s_s, *, blk, lam_init):
    i = pl.program_id(2)

    @pl.when(i == 0)
    def _():
        vx_s[:, :LANES] = v_ref[...]
        vx_s[:, LANES:] = jnp.ones((vx_s.shape[0], LANES), vx_s.dtype)

    q = q_ref[...]
    lane = lax.broadcasted_iota(jnp.int32, q.shape, 1)
    qs = q * (A_HEAD_DIM ** -0.5)
    zero = jnp.zeros_like(qs)
    q2_s[:blk, :] = jnp.where(lane < A_HEAD_DIM, qs, zero)
    q2_s[blk:, :] = jnp.where(lane >= A_HEAD_DIM, qs, zero)
    m_s[...] = jnp.full(m_s.shape, NEG_BIG, F32)
    acc_s[...] = jnp.zeros(acc_s.shape, F32)

    def scores(j):
        return _dot_nt(q2_s[...], k_ref[pl.ds(pl.multiple_of(j * blk, blk), blk), :])

    s_s[0] = scores(0)

    def step(j, s, diagonal):
        start = pl.multiple_of(j * blk, blk)
        if diagonal:
            row = lax.broadcasted_iota(jnp.int32, s.shape, 0)
            row = jnp.where(row >= blk, row - blk, row)
            col = lax.broadcasted_iota(jnp.int32, s.shape, 1)
            s = jnp.where(col <= row, s, NEG_BIG)
        m_prev = m_s[...]
        m_new = jnp.maximum(m_prev, jnp.max(s, axis=-1, keepdims=True))
        alpha = jnp.exp(m_prev - m_new)
        p = jnp.concatenate([jnp.exp(s[:, c:c + LANES] - m_new) for c in range(0, blk, LANES)], axis=1)
        pv = _dot(p.astype(BF16), vx_s[pl.ds(start, blk), :])
        acc_s[...] = jnp.concatenate([alpha, alpha], axis=1) * acc_s[...] + pv
        m_s[...] = m_new

    def pair(jj, carry):
        j = 2 * jj
        s_s[1] = scores(j + 1)
        step(j, s_s[0], False)
        s_s[0] = scores(j + 2)
        step(j + 1, s_s[1], False)
        return carry

    lax.fori_loop(0, i // 2, pair, 0)

    @pl.when(i % 2 == 1)
    def _():
        s_s[1] = scores(i)
        step(i - 1, s_s[0], False)

    step(i, s_s[i % 2], True)

    acc = acc_s[...]
    o = acc[:blk, :LANES] / acc[:blk, LANES:] - lam_ref[0] * (acc[blk:, :LANES] / acc[blk:, LANES:])
    o_ref[...] = (_rms(o, g_ref[...]) * (1.0 - lam_init)).astype(o_ref.dtype)


def diff_attention(z3, lam, subln_g, lam_init, *, blk):
    b, s, n = z3.shape
    heads = n // 3 // LANES
    kern = functools.partial(_diffattn_kernel, blk=blk, lam_init=lam_init)
    return pl.pallas_call(
        kern,
        out_shape=jax.ShapeDtypeStruct((b, s, heads * LANES), BF16),
        grid=(b, heads, s // blk),
        in_specs=[
            pl.BlockSpec(memory_space=pltpu.SMEM),
            pl.BlockSpec((None, blk, LANES), lambda bi, h, i: (bi, i, h)),
            pl.BlockSpec((None, s, LANES), lambda bi, h, i: (bi, 0, heads + h)),
            pl.BlockSpec((None, s, LANES), lambda bi, h, i: (bi, 0, 2 * heads + h)),
            pl.BlockSpec((1, LANES), lambda bi, h, i: (0, 0)),
        ],
        out_specs=pl.BlockSpec((None, blk, LANES), lambda bi, h, i: (bi, i, h)),
        scratch_shapes=[pltpu.VMEM((2 * blk, LANES), BF16), pltpu.VMEM((s, 2 * LANES), BF16),
                        pltpu.VMEM((2 * blk, LANES), F32), pltpu.VMEM((2 * blk, 2 * LANES), F32),
                        pltpu.VMEM((2, 2 * blk, blk), F32)],
        compiler_params=_params("arbitrary", "arbitrary", "arbitrary"),
        name="diff_attention",
    )(lam.reshape(1).astype(F32), z3, z3, z3, subln_g.reshape(1, LANES).astype(F32))


def _s5_tables(lam_re, lam_im, log_step, b_re, b_im, c_re, c_im, n_chunks):
    hp = lax.Precision.HIGHEST
    g_cnt, n = lam_re.shape
    ch = b_re.shape[-1]
    big_l = S5_CHUNK
    gb = S5_BLOCK_GROUPS
    blocks = g_cnt // gb
    dt = jnp.exp(log_step.astype(F32))[:, None]
    lr, li = lam_re.astype(F32), lam_im.astype(F32)
    mag = jnp.exp(lr * dt)
    ang = li * dt
    a_re, a_im = mag * jnp.cos(ang), mag * jnp.sin(ang)
    den = lr * lr + li * li
    num_re, num_im = a_re - 1.0, a_im
    fz_re = (num_re * lr + num_im * li) / den
    fz_im = (num_im * lr - num_re * li) / den
    br, bi = b_re.astype(F32), b_im.astype(F32)
    bb_re = fz_re[..., None] * br - fz_im[..., None] * bi
    bb_im = fz_re[..., None] * bi + fz_im[..., None] * br

    pr, pi = [jnp.ones_like(a_re)], [jnp.zeros_like(a_im)]
    for _ in range(big_l):
        pr.append(pr[-1] * a_re - pi[-1] * a_im)
        pi.append(pr[-2] * a_im + pi[-1] * a_re)
    pw_re, pw_im = jnp.stack(pr), jnp.stack(pi)

    p_re = pw_re[:big_l, :, :, None] * bb_re[None] - pw_im[:big_l, :, :, None] * bb_im[None]
    p_im = pw_re[:big_l, :, :, None] * bb_im[None] + pw_im[:big_l, :, :, None] * bb_re[None]
    cr, ci = c_re.astype(F32), c_im.astype(F32)
    k_lag = (jnp.einsum('gcn,jgnd->jgcd', cr, p_re, precision=hp)
             - jnp.einsum('gcn,jgnd->jgcd', ci, p_im, precision=hp))
    eye = jnp.eye(gb, dtype=F32)

    def blockdiag(w, rows, cols):
        lead = w.shape[:-3]
        wb = w.reshape(*lead, blocks, gb, rows, cols)
        out = wb[..., :, :, None, :] * eye[:, None, :, None]
        return out.reshape(*lead, blocks, gb * rows, gb * cols)

    k_bd = blockdiag(k_lag.transpose(0, 1, 3, 2), ch, ch)
    lag_tab = jnp.concatenate([k_bd[::-1].transpose(1, 0, 2, 3).reshape(blocks, big_l * gb * ch, gb * ch),
                               jnp.zeros((blocks, gb * ch, gb * ch), F32)], axis=1)

    def w_in_half(p):
        w = blockdiag(p[::-1].transpose(0, 1, 3, 2), ch, n)
        return w.transpose(1, 0, 2, 3).reshape(blocks, big_l * gb * ch, gb * n)

    w_in = jnp.concatenate([w_in_half(p_re), w_in_half(p_im)], axis=2)

    ar, ai = pw_re[1:], pw_im[1:]
    wo_re = (cr[None] * ar[:, :, None, :] - ci[None] * ai[:, :, None, :])
    wo_im = -(cr[None] * ai[:, :, None, :] + ci[None] * ar[:, :, None, :])

    def w_out_half(w):
        wb = blockdiag(w.transpose(0, 1, 3, 2), n, ch)
        return wb.transpose(1, 2, 0, 3).reshape(blocks, gb * n, big_l * gb * ch)

    w_out = jnp.concatenate([w_out_half(wo_re), w_out_half(wo_im)], axis=1)

    sr, si = [pw_re[big_l]], [pw_im[big_l]]
    steps = max(1, int(math.ceil(math.log2(n_chunks))))
    for _ in range(steps - 1):
        sr.append(sr[-1] * sr[-1] - si[-1] * si[-1])
        si.append(2.0 * sr[-2] * si[-1])
    scan_pw = jnp.stack(sr + si, axis=1).reshape(blocks, gb, 2 * steps, n)
    scan_pw = scan_pw.transpose(0, 2, 1, 3).reshape(blocks, 2 * steps, gb * n)
    return lag_tab.astype(BF16), w_in.astype(BF16), w_out.astype(BF16), scan_pw, steps


def _s5_kernel(u_ref, lag_ref, win_ref, wout_ref, pw_ref, y_ref, *, steps):
    big_l = u_ref.shape[0]
    ucat = jnp.concatenate([u_ref[l] for l in range(big_l)], axis=1)
    inj = _dot(ucat, win_ref[...])
    nb = inj.shape[1] // 2
    xr, xi = inj[:, :nb], inj[:, nb:]
    chunk = lax.broadcasted_iota(jnp.int32, xr.shape, 0)
    for t in range(steps):
        d = 1 << t
        ar = pw_ref[t:t + 1, :]
        ai = pw_ref[steps + t:steps + t + 1, :]
        sr = pltpu.roll(xr, d, 0)
        si = pltpu.roll(xi, d, 0)
        ok = chunk >= d
        xr, xi = (xr + jnp.where(ok, ar * sr - ai * si, 0.0),
                  xi + jnp.where(ok, ar * si + ai * sr, 0.0))
    first = chunk == 0
    xsr = jnp.where(first, 0.0, pltpu.roll(xr, 1, 0))
    xsi = jnp.where(first, 0.0, pltpu.roll(xi, 1, 0))
    y_state = _dot(jnp.concatenate([xsr, xsi], axis=1).astype(BF16), wout_ref[...])
    for m in range(big_l // 2):
        even = lag_ref[(big_l - 1 - 2 * m) * LANES:(big_l + 1) * LANES, :]
        odd = lag_ref[(big_l - 2 - 2 * m) * LANES:big_l * LANES, :]
        w = jnp.concatenate([even, odd], axis=1)
        y2 = _dot(ucat[:, :(2 * m + 2) * LANES], w) + y_state[:, 2 * m * LANES:(2 * m + 2) * LANES]
        y_ref[2 * m] = y2[:, :LANES].astype(y_ref.dtype)
        y_ref[2 * m + 1] = y2[:, LANES:].astype(y_ref.dtype)


def s5_scan(u_cm, tables, n_chunks):
    lag_tab, w_in, w_out, scan_pw, steps = tables
    big_l, blocks, rows, _ = u_cm.shape
    nb2 = 2 * S5_BLOCK_STATE
    kern = functools.partial(_s5_kernel, steps=steps)
    plane = pl.BlockSpec((big_l, None, n_chunks, LANES), lambda p, b: (0, p, b, 0))
    return pl.pallas_call(
        kern,
        out_shape=jax.ShapeDtypeStruct(u_cm.shape, BF16),
        grid=(blocks, rows // n_chunks),
        in_specs=[
            plane,
            pl.BlockSpec((None, (big_l + 1) * LANES, LANES), lambda p, b: (p, 0, 0)),
            pl.BlockSpec((None, big_l * LANES, nb2), lambda p, b: (p, 0, 0)),
            pl.BlockSpec((None, nb2, big_l * LANES), lambda p, b: (p, 0, 0)),
            pl.BlockSpec((None, 2 * steps, S5_BLOCK_STATE), lambda p, b: (p, 0, 0)),
        ],
        out_specs=plane,
        compiler_params=_params("parallel", "arbitrary"),
        name="s5_scan",
    )(u_cm, lag_tab, w_in, w_out, scan_pw)


def _s5_glu_kernel(y_ref, u_ref, d_ref, w_ref, o_ref):
    blocks = y_ref.shape[0]
    y = jnp.concatenate([y_ref[p] for p in range(blocks)], axis=1).astype(F32)
    u = jnp.concatenate([u_ref[p] for p in range(blocks)], axis=1).astype(F32)
    y = y + d_ref[...] * u
    z = 0.5 * y * (1.0 + jnp.tanh(math.sqrt(2.0 / math.pi) * (y + 0.044715 * (y * y * y))))
    gate = _dot(z.astype(BF16), w_ref[...])
    o_ref[...] = (z * _sigmoid(gate)).astype(o_ref.dtype)


def s5_glu(y_cm, u_cm, d_skip, w_glu, *, rows=512):
    big_l, blocks, r, _ = y_cm.shape
    dh = blocks * LANES
    plane = pl.BlockSpec((None, blocks, rows, LANES), lambda l, i: (l, 0, i, 0))
    out = pl.pallas_call(
        _s5_glu_kernel,
        out_shape=jax.ShapeDtypeStruct((r, big_l * dh), BF16),
        grid=(big_l, r // rows),
        in_specs=[plane, plane, pl.BlockSpec((1, dh), lambda l, i: (0, 0)),
                  pl.BlockSpec((dh, dh), lambda l, i: (0, 0))],
        out_specs=pl.BlockSpec((rows, dh), lambda l, i: (i, l)),
        compiler_params=_params("parallel", "parallel"),
        name="s5_glu",
    )(y_cm, u_cm, d_skip.reshape(1, dh).astype(F32), w_glu)
    return out.reshape(r * big_l, dh)


CONV_ROWS = 64
CONV_COLS = 256


def _conv_kernel(a_ref, g_ref, w_ref, b_ref, lng_ref, lnb_ref, o_ref, h_s, sh_s, y_s, *, rows):
    i = pl.program_id(1)

    @pl.when(i == 0)
    def _():
        h_s[0:CONV_HALO, :] = jnp.zeros((CONV_HALO, h_s.shape[1]), F32)

    @pl.when(i > 0)
    def _():
        h_s[0:CONV_HALO, :] = h_s[rows:rows + CONV_HALO, :]

    h_s[CONV_HALO:CONV_HALO + rows, :] = a_ref[...].astype(F32) * _sigmoid(g_ref[...].astype(F32))
    first_tap = CONV_HALO - (CONV_WIDTH - 1)
    shifted_rows = sh_s.shape[1]

    def col_block(cb, carry):
        c0 = pl.multiple_of(cb * CONV_COLS, CONV_COLS)
        cols = pl.ds(c0, CONV_COLS)
        for k in range(1, SUBLANES):
            sh_s[k - 1, :, cols] = h_s[k:k + shifted_rows, cols]
        for r0 in range(0, rows, CONV_ROWS):
            acc = jnp.zeros((CONV_ROWS, CONV_COLS), F32)
            for j in range(CONV_WIDTH):
                k = (first_tap + j) % SUBLANES
                base = r0 + first_tap + j - k
                window = (h_s[base:base + CONV_ROWS, cols] if k == 0
                          else sh_s[k - 1, base:base + CONV_ROWS, cols])
                acc = acc + w_ref[j:j + 1, cols] * window
            y_s[r0:r0 + CONV_ROWS, cols] = acc + b_ref[:, cols]
        return carry

    lax.fori_loop(0, h_s.shape[1] // CONV_COLS, col_block, 0)
    y = y_s[...]
    mu = jnp.mean(y, axis=-1, keepdims=True)
    yc = y - mu
    var = jnp.mean(yc * yc, axis=-1, keepdims=True)
    yn = yc * lax.rsqrt(var + LN_EPS) * lng_ref[...] + lnb_ref[...]
    o_ref[...] = (yn * _sigmoid(yn)).astype(o_ref.dtype)


def conv_module(z3, w, bias, ln_g, ln_b, *, rows=256):
    b, s, n = z3.shape
    dh = n // 6
    kern = functools.partial(_conv_kernel, rows=rows)
    vec = lambda v: v.reshape(1, dh).astype(F32)
    vec_spec = pl.BlockSpec((1, dh), lambda bi, i: (0, 0))
    return pl.pallas_call(
        kern,
        out_shape=jax.ShapeDtypeStruct((b, s, dh), BF16),
        grid=(b, s // rows),
        in_specs=[pl.BlockSpec((None, rows, dh), lambda bi, i: (bi, i, 0)),
                  pl.BlockSpec((None, rows, dh), lambda bi, i: (bi, i, 1)),
                  pl.BlockSpec((CONV_WIDTH, dh), lambda bi, i: (0, 0)), vec_spec, vec_spec, vec_spec],
        out_specs=pl.BlockSpec((None, rows, dh), lambda bi, i: (bi, i, 0)),
        scratch_shapes=[pltpu.VMEM((CONV_HALO + rows, dh), F32),
                        pltpu.VMEM((SUBLANES - 1, CONV_HALO + rows - SUBLANES, dh), F32),
                        pltpu.VMEM((rows, dh), F32)],
        compiler_params=_params("arbitrary", "arbitrary"),
        name="conv_module",
    )(z3, z3, w.astype(F32), vec(bias), vec(ln_g), vec(ln_b))


H_HEADS_PER_STEP = 4


def _split3(x):
    hi = x.astype(BF16)
    r1 = x - hi.astype(F32)
    mid = r1.astype(BF16)
    lo = (r1 - mid.astype(F32)).astype(BF16)
    return hi, mid, lo


def _hgrn_kernel(q_ref, f_ref, i_ref, g_ref, lb_ref, ng_ref, o_ref, st_s, *, rows):
    @pl.when(pl.program_id(2) == 0)
    def _():
        st_s[...] = jnp.zeros(st_s.shape, F32)

    big_l = H_CHUNK
    dk = H_EXPAND
    n_chunks = rows // big_l
    r_i = lax.broadcasted_iota(jnp.int32, (rows, rows), 0)
    c_i = lax.broadcasted_iota(jnp.int32, (rows, rows), 1)
    causal = jnp.logical_and(c_i <= r_i, r_i - c_i <= r_i % big_l)
    tri = jnp.where(causal, 1.0, 0.0).astype(BF16)

    def per_chunk_rows(x, row):
        return jnp.concatenate([jnp.broadcast_to(x[c * big_l + row:c * big_l + row + 1, :], (big_l, dk))
                                for c in range(n_chunks)], axis=0)

    for hh in range(st_s.shape[0]):
        cs = slice(hh * dk, (hh + 1) * dk)
        lb = lb_ref[:, cs]
        q = q_ref[:, cs].astype(F32)
        q = q * _sigmoid(q)
        fg = lb + (1.0 - lb) * _sigmoid(f_ref[:, cs].astype(F32))
        log_f = jnp.log(fg)
        k = 1.0 - fg
        v = i_ref[:, cs]
        cum3 = _dot(tri, jnp.concatenate(_split3(log_f), axis=1))
        bcum = cum3[:, :dk] + cum3[:, dk:2 * dk] + cum3[:, 2 * dk:]
        b_mid = per_chunk_rows(bcum, big_l // 2 - 1)
        b_last = per_chunk_rows(bcum, big_l - 1)
        q_in = (q * jnp.exp(bcum - b_mid)).astype(BF16)
        k_in = (k * jnp.exp(b_mid - bcum)).astype(BF16)
        scores = jnp.where(causal, _dot_nt(q_in, k_in), 0.0)
        o_intra = _dot(scores.astype(BF16), v)
        q_dec = (q * jnp.exp(bcum)).astype(BF16)
        k_dec = (k * jnp.exp(b_last - bcum)).astype(BF16)
        st = st_s[hh]
        o_inter = []
        for c in range(n_chunks):
            sl = slice(c * big_l, (c + 1) * big_l)
            o_inter.append(_dot_nt(q_dec[sl], st.astype(BF16)))
            st = st * jnp.exp(bcum[(c + 1) * big_l - 1:(c + 1) * big_l, :]) + _dot_tn(v[sl], k_dec[sl])
        st_s[hh] = st
        o = o_intra + jnp.concatenate(o_inter, axis=0)
        gate = g_ref[:, cs].astype(F32)
        o_ref[:, cs] = (_rms(o, ng_ref[...]) * (gate * _sigmoid(gate))).astype(o_ref.dtype)


def hgrn2(z3, lb, norm_g, *, rows=256):
    b, s, n = z3.shape
    dh = n // 6
    hb = min(H_HEADS_PER_STEP, dh // H_EXPAND)
    width = hb * H_EXPAND
    head_blocks = dh // width
    kern = functools.partial(_hgrn_kernel, rows=rows)

    def col(group):
        return pl.BlockSpec((None, rows, width), lambda bi, h, i: (bi, i, group * head_blocks + h))

    return pl.pallas_call(
        kern,
        out_shape=jax.ShapeDtypeStruct((b, s, dh), BF16),
        grid=(b, head_blocks, s // rows),
        in_specs=[col(2), col(3), col(4), col(5),
                  pl.BlockSpec((1, width), lambda bi, h, i: (0, h)),
                  pl.BlockSpec((1, H_EXPAND), lambda bi, h, i: (0, 0))],
        out_specs=pl.BlockSpec((None, rows, width), lambda bi, h, i: (bi, i, h)),
        scratch_shapes=[pltpu.VMEM((hb, H_EXPAND, H_EXPAND), F32)],
        compiler_params=_params("parallel", "parallel", "arbitrary"),
        name="hgrn2",
    )(z3, z3, z3, z3, lb.reshape(1, dh).astype(F32), norm_g.reshape(1, H_EXPAND).astype(F32))


def _tile(n, want):
    if n <= want:
        return n
    t = want - want % LANES
    while n % t:
        t -= LANES
    return t


def kernel(x, norm_gains, ffn_w_gate, ffn_w_up, ffn_w_down, ev_w_in, ev_w_out, diff_lambda, diff_subln, s5_lambda_re, s5_lambda_im, s5_log_step, s5_B_re, s5_B_im, s5_C_re, s5_C_im, s5_D, s5_w_glu, od_w_in, od_w_out, conv_w, conv_b, conv_ln_g, conv_ln_b, hgrn_lb, hgrn_norm_g):
    bsz, seq, d = x.shape
    depth = norm_gains.shape[0]
    t = bsz * seq
    dh = d // 2
    d_ff = ffn_w_gate.shape[-1]
    tm = _tile(t, 1024)
    attn_blk = _tile(seq, 512)
    n_chunks = seq // S5_CHUNK

    lb_all = jnp.cumsum(jax.nn.softmax(hgrn_lb.astype(F32), axis=0), axis=0)
    lb_all = lb_all - lb_all[0]

    xf = x.reshape(t, d)
    h = rmsnorm(xf, norm_gains[0, 0])
    for l in range(depth):
        gains = norm_gains[l]
        if l % 2 == 0:
            e = l // 2
            lam_init = 0.8 - 0.6 * math.exp(-0.3 * l)
            w_in = ev_w_in[e]
            qkv = matmul(h, w_in[:, :3 * dh].astype(BF16), tm=tm, tn=_tile(3 * dh, 1024))
            u_cm = matmul_chunk_major(h, w_in[:, 3 * dh:].astype(BF16), tn=_tile(dh, 1024))
            lf = diff_lambda[e].astype(F32)
            lam = jnp.exp(jnp.sum(lf[0] * lf[1])) - jnp.exp(jnp.sum(lf[2] * lf[3])) + lam_init
            o_a = diff_attention(qkv.reshape(bsz, seq, 3 * dh), lam, diff_subln[e], lam_init,
                                 blk=attn_blk).reshape(t, dh)
            tables = _s5_tables(s5_lambda_re[e], s5_lambda_im[e], s5_log_step[e], s5_B_re[e], s5_B_im[e],
                                s5_C_re[e], s5_C_im[e], n_chunks)
            y_cm = s5_scan(u_cm, tables, n_chunks)
            o_b = s5_glu(y_cm, u_cm, s5_D[e], s5_w_glu[e].astype(BF16), rows=_tile(t // S5_CHUNK, 512))
            mix = matmul_cat(o_a, o_b, ev_w_out[e].astype(BF16), tm=tm, tn=_tile(d, 1024))
        else:
            o = l // 2
            z = matmul(h, od_w_in[o].astype(BF16), tm=tm, tn=_tile(6 * dh, 1024))
            z3 = z.reshape(bsz, seq, 6 * dh)
            o_c = conv_module(z3, conv_w[o], conv_b[o], conv_ln_g[o], conv_ln_b[o]).reshape(t, dh)
            o_d = hgrn2(z3, lb_all[l], hgrn_norm_g[o]).reshape(t, dh)
            mix = matmul_cat(o_c, o_d, od_w_out[o].astype(BF16), tm=tm, tn=_tile(d, 1024))
        xf, h = resnorm(mix, xf, gains[1], gains[2])
        hidden = swiglu_gate_up(h, ffn_w_gate[l].astype(BF16), ffn_w_up[l].astype(BF16),
                                tm=_tile(t, 2048), tn=_tile(d_ff, 256))
        down = matmul(hidden, ffn_w_down[l].astype(BF16), tm=_tile(t, 512), tn=_tile(d, 512))
        xf, h = resnorm(down, xf, gains[3], norm_gains[l + 1, 0] if l + 1 < depth else None)
    return xf.reshape(bsz, seq, d)
```

```python
import functools
import math

import jax
import jax.numpy as jnp
from jax import lax
from jax.experimental import pallas as pl
from jax.experimental.pallas import tpu as pltpu

F32 = jnp.float32
BF16 = jnp.bfloat16

A_HEAD_DIM = 64
S5_GROUP = 16
S5_STATE = 64
S5_CHUNK = 16
CONV_WIDTH = 31
CONV_HALO = 32
H_EXPAND = 128
H_CHUNK = 64
RMS_EPS = 1e-6
LN_EPS = 1e-5
LANES = 128
SUBLANES = 8
VMEM_LIMIT_BYTES = 56 * 1024 * 1024
NEG_BIG = -1e30
S5_BLOCK_GROUPS = LANES // S5_GROUP
S5_BLOCK_STATE = S5_BLOCK_GROUPS * S5_STATE


def _params(*sem):
    return pltpu.CompilerParams(dimension_semantics=sem, vmem_limit_bytes=VMEM_LIMIT_BYTES)


def _sigmoid(x):
    return 1.0 / (1.0 + jnp.exp(-x))


def _dot(a, b):
    return jnp.dot(a, b, preferred_element_type=F32)


def _dot_nt(a, b):
    return lax.dot_general(a, b, (((1,), (1,)), ((), ())), preferred_element_type=F32)


def _dot_tn(a, b):
    return lax.dot_general(a, b, (((0,), (0,)), ((), ())), preferred_element_type=F32)


def _rms(x, g):
    return x * lax.rsqrt(jnp.mean(x * x, axis=-1, keepdims=True) + RMS_EPS) * g


def _rmsnorm_kernel(x_ref, g_ref, o_ref):
    o_ref[...] = _rms(x_ref[...].astype(F32), g_ref[...]).astype(o_ref.dtype)


def rmsnorm(x, g, *, rows=512):
    t, d = x.shape
    return pl.pallas_call(
        _rmsnorm_kernel,
        out_shape=jax.ShapeDtypeStruct((t, d), BF16),
        grid=(t // rows,),
        in_specs=[pl.BlockSpec((rows, d), lambda i: (i, 0)), pl.BlockSpec((1, d), lambda i: (0, 0))],
        out_specs=pl.BlockSpec((rows, d), lambda i: (i, 0)),
        compiler_params=_params("parallel"),
        name="rmsnorm",
    )(x, g.reshape(1, d).astype(F32))


def _resnorm_kernel(y_ref, x_ref, gpost_ref, gpre_ref, xo_ref, ho_ref):
    xn = x_ref[...] + _rms(y_ref[...].astype(F32), gpost_ref[...])
    xo_ref[...] = xn
    ho_ref[...] = _rms(xn, gpre_ref[...]).astype(ho_ref.dtype)


def _resnorm_last_kernel(y_ref, x_ref, gpost_ref, xo_ref):
    xo_ref[...] = x_ref[...] + _rms(y_ref[...].astype(F32), gpost_ref[...])


def resnorm(y, x, g_post, g_pre, *, rows=256):
    t, d = x.shape
    row_spec = pl.BlockSpec((rows, d), lambda i: (i, 0))
    gain_spec = pl.BlockSpec((1, d), lambda i: (0, 0))
    if g_pre is None:
        return pl.pallas_call(
            _resnorm_last_kernel,
            out_shape=jax.ShapeDtypeStruct((t, d), F32),
            grid=(t // rows,),
            in_specs=[row_spec, row_spec, gain_spec],
            out_specs=row_spec,
            compiler_params=_params("parallel"),
            name="resnorm_last",
        )(y, x, g_post.reshape(1, d).astype(F32)), None
    return pl.pallas_call(
        _resnorm_kernel,
        out_shape=(jax.ShapeDtypeStruct((t, d), F32), jax.ShapeDtypeStruct((t, d), BF16)),
        grid=(t // rows,),
        in_specs=[row_spec, row_spec, gain_spec, gain_spec],
        out_specs=(row_spec, row_spec),
        compiler_params=_params("parallel"),
        name="resnorm",
    )(y, x, g_post.reshape(1, d).astype(F32), g_pre.reshape(1, d).astype(F32))


def _mm_kernel(x_ref, w_ref, o_ref):
    o_ref[...] = _dot(x_ref[...], w_ref[...].astype(BF16)).astype(o_ref.dtype)


def _weight_spec(w, k, tn, layer, col_start, row_block=0):
    if w.ndim == 2:
        return pl.BlockSpec((k, tn), lambda i, j: (row_block, j + col_start))
    return pl.BlockSpec((None, k, tn), lambda i, j: (layer, row_block, j + col_start))


def matmul(x, w, *, tm, tn, out_dtype=BF16, layer=0, col_start=0, n=None):
    m, k = x.shape
    n = w.shape[-1] if n is None else n
    return pl.pallas_call(
        _mm_kernel,
        out_shape=jax.ShapeDtypeStruct((m, n), out_dtype),
        grid=(m // tm, n // tn),
        in_specs=[pl.BlockSpec((tm, k), lambda i, j: (i, 0)), _weight_spec(w, k, tn, layer, col_start)],
        out_specs=pl.BlockSpec((tm, tn), lambda i, j: (i, j)),
        compiler_params=_params("parallel", "arbitrary"),
        name="matmul",
    )(x, w)


def _mm_chunk_major_kernel(x_ref, w_ref, o_ref):
    acc = _dot(x_ref[...], w_ref[...].astype(BF16))
    for p in range(o_ref.shape[0]):
        o_ref[p] = acc[:, p * LANES:(p + 1) * LANES].astype(o_ref.dtype)


def matmul_chunk_major(x, w, *, tn, n, layer=0, col_start=0):
    t, k = x.shape
    big_l = S5_CHUNK
    r = t // big_l
    return pl.pallas_call(
        _mm_chunk_major_kernel,
        out_shape=jax.ShapeDtypeStruct((big_l, n // LANES, r, LANES), BF16),
        grid=(big_l, n // tn),
        in_specs=[pl.BlockSpec((r, k), lambda l, j: (0, l)), _weight_spec(w, k, tn, layer, col_start)],
        out_specs=pl.BlockSpec((None, tn // LANES, r, LANES), lambda l, j: (l, j, 0, 0)),
        compiler_params=_params("parallel", "arbitrary"),
        name="matmul_chunk_major",
    )(x.reshape(r, big_l * k), w)


def _mm2_kernel(x1_ref, x2_ref, w1_ref, w2_ref, o_ref):
    o_ref[...] = (_dot(x1_ref[...], w1_ref[...].astype(BF16))
                  + _dot(x2_ref[...], w2_ref[...].astype(BF16))).astype(o_ref.dtype)


def matmul_cat(x1, x2, w, *, tm, tn, out_dtype=BF16, layer=0):
    m, kh = x1.shape
    n = w.shape[-1]
    return pl.pallas_call(
        _mm2_kernel,
        out_shape=jax.ShapeDtypeStruct((m, n), out_dtype),
        grid=(m // tm, n // tn),
        in_specs=[pl.BlockSpec((tm, kh), lambda i, j: (i, 0)), pl.BlockSpec((tm, kh), lambda i, j: (i, 0)),
                  _weight_spec(w, kh, tn, layer, 0, row_block=0), _weight_spec(w, kh, tn, layer, 0, row_block=1)],
        out_specs=pl.BlockSpec((tm, tn), lambda i, j: (i, j)),
        compiler_params=_params("parallel", "arbitrary"),
        name="matmul_cat",
    )(x1, x2, w, w)


def _gateup_kernel(x_ref, wg_ref, wu_ref, o_ref):
    x = x_ref[...]
    g = _dot(x, wg_ref[...].astype(BF16))
    u = _dot(x, wu_ref[...].astype(BF16))
    o_ref[...] = (g * _sigmoid(g) * u).astype(o_ref.dtype)


def swiglu_gate_up(x, wg, wu, *, tm, tn, layer=0):
    m, k = x.shape
    n = wg.shape[-1]
    w_spec = _weight_spec(wg, k, tn, layer, 0)
    return pl.pallas_call(
        _gateup_kernel,
        out_shape=jax.ShapeDtypeStruct((m, n), BF16),
        grid=(m // tm, n // tn),
        in_specs=[pl.BlockSpec((tm, k), lambda i, j: (i, 0)), w_spec, w_spec],
        out_specs=pl.BlockSpec((tm, tn), lambda i, j: (i, j)),
        compiler_params=_params("parallel", "arbitrary"),
        name="swiglu_gate_up",
    )(x, wg, wu)


def _diffattn_kernel(lam_ref, q_ref, k_ref, v_ref, g_ref, o_ref, q2_s, vx_s, m_s, acc_s, s_s, *, blk, lam_init):
    i = pl.program_id(2)

    @pl.when(i == 0)
    def _():
        vx_s[:, :LANES] = v_ref[...]
        vx_s[:, LANES:] = jnp.ones((vx_s.shape[0], LANES), vx_s.dtype)

    q = q_ref[...]
    lane = lax.broadcasted_iota(jnp.int32, q.shape, 1)
    qs = q * (A_HEAD_DIM ** -0.5)
    zero = jnp.zeros_like(qs)
    q2_s[:blk, :] = jnp.where(lane < A_HEAD_DIM, qs, zero)
    q2_s[blk:, :] = jnp.where(lane >= A_HEAD_DIM, qs, zero)
    m_s[...] = jnp.full(m_s.shape, NEG_BIG, F32)
    acc_s[...] = jnp.zeros(acc_s.shape, F32)

    def scores(j):
        return _dot_nt(q2_s[...], k_ref[pl.ds(pl.multiple_of(j * blk, blk), blk), :])

    s_s[0] = scores(0)

    def step(j, s, diagonal):
        start = pl.multiple_of(j * blk, blk)
        if diagonal:
            row = lax.broadcasted_iota(jnp.int32, s.shape, 0)
            row = jnp.where(row >= blk, row - blk, row)
            col = lax.broadcasted_iota(jnp.int32, s.shape, 1)
            s = jnp.where(col <= row, s, NEG_BIG)
        m_prev = m_s[...]
        m_new = jnp.maximum(m_prev, jnp.max(s, axis=-1, keepdims=True))
        alpha = jnp.exp(m_prev - m_new)
        p = jnp.concatenate([jnp.exp(s[:, c:c + LANES] - m_new) for c in range(0, blk, LANES)], axis=1)
        pv = _dot(p.astype(BF16), vx_s[pl.ds(start, blk), :])
        acc_s[...] = jnp.concatenate([alpha, alpha], axis=1) * acc_s[...] + pv
        m_s[...] = m_new

    def pair(jj, carry):
        j = 2 * jj
        s_s[1] = scores(j + 1)
        step(j, s_s[0], False)
        s_s[0] = scores(j + 2)
        step(j + 1, s_s[1], False)
        return carry

    lax.fori_loop(0, i // 2, pair, 0)

    @pl.when(i % 2 == 1)
    def _():
        s_s[1] = scores(i)
        step(i - 1, s_s[0], False)

    step(i, s_s[i % 2], True)

    acc = acc_s[...]
    o = acc[:blk, :LANES] / acc[:blk, LANES:] - lam_ref[0] * (acc[blk:, :LANES] / acc[blk:, LANES:])
    o_ref[...] = (_rms(o, g_ref[...]) * (1.0 - lam_init)).astype(o_ref.dtype)


def diff_attention(z3, lam, subln_g, lam_init, *, blk):
    b, s, n = z3.shape
    heads = n // 3 // LANES
    kern = functools.partial(_diffattn_kernel, blk=blk, lam_init=lam_init)
    return pl.pallas_call(
        kern,
        out_shape=jax.ShapeDtypeStruct((b, s, heads * LANES), BF16),
        grid=(b, heads, s // blk),
        in_specs=[
            pl.BlockSpec(memory_space=pltpu.SMEM),
            pl.BlockSpec((None, blk, LANES), lambda bi, h, i: (bi, i, h)),
            pl.BlockSpec((None, s, LANES), lambda bi, h, i: (bi, 0, heads + h)),
            pl.BlockSpec((None, s, LANES), lambda bi, h, i: (bi, 0, 2 * heads + h)),
            pl.BlockSpec((1, LANES), lambda bi, h, i: (0, 0)),
        ],
        out_specs=pl.BlockSpec((None, blk, LANES), lambda bi, h, i: (bi, i, h)),
        scratch_shapes=[pltpu.VMEM((2 * blk, LANES), BF16), pltpu.VMEM((s, 2 * LANES), BF16),
                        pltpu.VMEM((2 * blk, LANES), F32), pltpu.VMEM((2 * blk, 2 * LANES), F32),
                        pltpu.VMEM((2, 2 * blk, blk), F32)],
        compiler_params=_params("arbitrary", "arbitrary", "arbitrary"),
        name="diff_attention",
    )(lam.reshape(1).astype(F32), z3, z3, z3, subln_g.reshape(1, LANES).astype(F32))


def _s5_tables(lam_re, lam_im, log_step, b_re, b_im, c_re, c_im, n_chunks):
    hp = lax.Precision.HIGHEST
    g_cnt, n = lam_re.shape
    ch = b_re.shape[-1]
    big_l = S5_CHUNK
    gb = S5_BLOCK_GROUPS
    blocks = g_cnt // gb
    dt = jnp.exp(log_step.astype(F32))[:, None]
    lr, li = lam_re.astype(F32), lam_im.astype(F32)
    mag = jnp.exp(lr * dt)
    ang = li * dt
    a_re, a_im = mag * jnp.cos(ang), mag * jnp.sin(ang)
    den = lr * lr + li * li
    num_re, num_im = a_re - 1.0, a_im
    fz_re = (num_re * lr + num_im * li) / den
    fz_im = (num_im * lr - num_re * li) / den
    br, bi = b_re.astype(F32), b_im.astype(F32)
    bb_re = fz_re[..., None] * br - fz_im[..., None] * bi
    bb_im = fz_re[..., None] * bi + fz_im[..., None] * br

    pr, pi = [jnp.ones_like(a_re)], [jnp.zeros_like(a_im)]
    for _ in range(big_l):
        pr.append(pr[-1] * a_re - pi[-1] * a_im)
        pi.append(pr[-2] * a_im + pi[-1] * a_re)
    pw_re, pw_im = jnp.stack(pr), jnp.stack(pi)

    p_re = pw_re[:big_l, :, :, None] * bb_re[None] - pw_im[:big_l, :, :, None] * bb_im[None]
    p_im = pw_re[:big_l, :, :, None] * bb_im[None] + pw_im[:big_l, :, :, None] * bb_re[None]
    cr, ci = c_re.astype(F32), c_im.astype(F32)
    k_lag = (jnp.einsum('gcn,jgnd->jgcd', cr, p_re, precision=hp)
             - jnp.einsum('gcn,jgnd->jgcd', ci, p_im, precision=hp))
    eye = jnp.eye(gb, dtype=F32)

    def blockdiag(w, rows, cols):
        lead = w.shape[:-3]
        wb = w.reshape(*lead, blocks, gb, rows, cols)
        out = wb[..., :, :, None, :] * eye[:, None, :, None]
        return out.reshape(*lead, blocks, gb * rows, gb * cols)

    k_bd = blockdiag(k_lag.transpose(0, 1, 3, 2), ch, ch)
    lag_tab = jnp.concatenate([k_bd[::-1].transpose(1, 0, 2, 3).reshape(blocks, big_l * gb * ch, gb * ch),
                               jnp.zeros((blocks, gb * ch, gb * ch), F32)], axis=1)

    pairs = gb // 2
    eye2 = jnp.eye(2, dtype=F32)

    def w_in_half(p):
        w = p[::-1].reshape(big_l, blocks, pairs, 2, n, ch).transpose(1, 0, 2, 3, 5, 4)
        w = w[..., None, :] * eye2[:, None, :, None]
        return w.reshape(blocks, big_l * gb * ch, 2 * n)

    w_in = jnp.concatenate([w_in_half(p_re), w_in_half(p_im)], axis=2)

    ar, ai = pw_re[1:], pw_im[1:]
    wo_re = (cr[None] * ar[:, :, None, :] - ci[None] * ai[:, :, None, :])
    wo_im = -(cr[None] * ai[:, :, None, :] + ci[None] * ar[:, :, None, :])

    def w_out_half(w):
        w = w.reshape(big_l, blocks, pairs, 2, ch, n).transpose(1, 5, 0, 2, 3, 4)
        w = w[:, None] * eye2[:, None, None, None, :, None]
        return w.reshape(blocks, 2 * n, big_l * gb * ch)

    w_out = jnp.concatenate([w_out_half(wo_re), w_out_half(wo_im)], axis=1)

    sr, si = [pw_re[big_l]], [pw_im[big_l]]
    steps = max(1, int(math.ceil(math.log2(n_chunks))))
    for _ in range(steps - 1):
        sr.append(sr[-1] * sr[-1] - si[-1] * si[-1])
        si.append(2.0 * sr[-2] * si[-1])
    scan_pw = jnp.stack(sr + si, axis=1).reshape(blocks, gb, 2 * steps, n)
    scan_pw = scan_pw.transpose(0, 2, 1, 3).reshape(blocks, 2 * steps, gb * n)
    return lag_tab.astype(BF16), w_in.astype(BF16), w_out.astype(BF16), scan_pw, steps


def _s5_kernel(u_ref, lag_ref, win_ref, wout_ref, pw_ref, y_ref, win_s, wout_s, *, steps):
    big_l = u_ref.shape[0]
    pairs = S5_BLOCK_GROUPS // 2
    pair_width = 2 * S5_GROUP

    @pl.when(pl.program_id(1) == 0)
    def _():
        row_pair = (lax.broadcasted_iota(jnp.int32, (win_ref.shape[0], LANES), 0) // pair_width) % pairs
        col_pair = (lax.broadcasted_iota(jnp.int32, (LANES, wout_ref.shape[1]), 1) // pair_width) % pairs
        for k in range(pairs):
            keep_rows = jnp.where(row_pair == k, 1.0, 0.0).astype(BF16)
            keep_cols = jnp.where(col_pair == k, 1.0, 0.0).astype(BF16)
            for half in range(2):
                src = slice(half * LANES, (half + 1) * LANES)
                dst = slice((half * pairs + k) * LANES, (half * pairs + k + 1) * LANES)
                win_s[:, dst] = win_ref[:, src] * keep_rows
                wout_s[dst, :] = wout_ref[src, :] * keep_cols

    ucat = jnp.concatenate([u_ref[l] for l in range(big_l)], axis=1)
    inj = _dot(ucat, win_s[...])
    nb = inj.shape[1] // 2
    xr, xi = inj[:, :nb], inj[:, nb:]
    chunk = lax.broadcasted_iota(jnp.int32, xr.shape, 0)
    for t in range(steps):
        d = 1 << t
        ar = pw_ref[t:t + 1, :]
        ai = pw_ref[steps + t:steps + t + 1, :]
        sr = pltpu.roll(xr, d, 0)
        si = pltpu.roll(xi, d, 0)
        ok = chunk >= d
        xr, xi = (xr + jnp.where(ok, ar * sr - ai * si, 0.0),
                  xi + jnp.where(ok, ar * si + ai * sr, 0.0))
    first = chunk == 0
    xsr = jnp.where(first, 0.0, pltpu.roll(xr, 1, 0))
    xsi = jnp.where(first, 0.0, pltpu.roll(xi, 1, 0))
    y_state = _dot(jnp.concatenate([xsr, xsi], axis=1).astype(BF16), wout_s[...])
    for m in range(big_l // 2):
        even = lag_ref[(big_l - 1 - 2 * m) * LANES:(big_l + 1) * LANES, :]
        odd = lag_ref[(big_l - 2 - 2 * m) * LANES:big_l * LANES, :]
        w = jnp.concatenate([even, odd], axis=1)
        y2 = _dot(ucat[:, :(2 * m + 2) * LANES], w) + y_state[:, 2 * m * LANES:(2 * m + 2) * LANES]
        y_ref[2 * m] = y2[:, :LANES].astype(y_ref.dtype)
        y_ref[2 * m + 1] = y2[:, LANES:].astype(y_ref.dtype)


def s5_scan(u_cm, tables, n_chunks):
    lag_tab, w_in, w_out, scan_pw, steps = tables
    big_l, blocks, rows, _ = u_cm.shape
    nb2 = 2 * S5_BLOCK_STATE
    kern = functools.partial(_s5_kernel, steps=steps)
    plane = pl.BlockSpec((big_l, None, n_chunks, LANES), lambda p, b: (0, p, b, 0))
    return pl.pallas_call(
        kern,
        out_shape=jax.ShapeDtypeStruct(u_cm.shape, BF16),
        grid=(blocks, rows // n_chunks),
        in_specs=[
            plane,
            pl.BlockSpec((None, (big_l + 1) * LANES, LANES), lambda p, b: (p, 0, 0)),
            pl.BlockSpec((None, big_l * LANES, 2 * LANES), lambda p, b: (p, 0, 0)),
            pl.BlockSpec((None, 2 * LANES, big_l * LANES), lambda p, b: (p, 0, 0)),
            pl.BlockSpec((None, 2 * steps, S5_BLOCK_STATE), lambda p, b: (p, 0, 0)),
        ],
        out_specs=plane,
        scratch_shapes=[pltpu.VMEM((big_l * LANES, nb2), BF16), pltpu.VMEM((nb2, big_l * LANES), BF16)],
        compiler_params=_params("parallel", "arbitrary"),
        name="s5_scan",
    )(u_cm, lag_tab, w_in, w_out, scan_pw)


def _s5_glu_kernel(y_ref, u_ref, d_ref, w_ref, o_ref):
    blocks = y_ref.shape[0]
    y = jnp.concatenate([y_ref[p] for p in range(blocks)], axis=1).astype(F32)
    u = jnp.concatenate([u_ref[p] for p in range(blocks)], axis=1).astype(F32)
    y = y + d_ref[...] * u
    z = 0.5 * y * (1.0 + jnp.tanh(math.sqrt(2.0 / math.pi) * (y + 0.044715 * (y * y * y))))
    gate = _dot(z.astype(BF16), w_ref[...])
    o_ref[...] = (z * _sigmoid(gate)).astype(o_ref.dtype)


def s5_glu(y_cm, u_cm, d_skip, w_glu, *, rows=512):
    big_l, blocks, r, _ = y_cm.shape
    dh = blocks * LANES
    plane = pl.BlockSpec((None, blocks, rows, LANES), lambda l, i: (l, 0, i, 0))
    out = pl.pallas_call(
        _s5_glu_kernel,
        out_shape=jax.ShapeDtypeStruct((r, big_l * dh), BF16),
        grid=(big_l, r // rows),
        in_specs=[plane, plane, pl.BlockSpec((1, dh), lambda l, i: (0, 0)),
                  pl.BlockSpec((dh, dh), lambda l, i: (0, 0))],
        out_specs=pl.BlockSpec((rows, dh), lambda l, i: (i, l)),
        compiler_params=_params("parallel", "parallel"),
        name="s5_glu",
    )(y_cm, u_cm, d_skip.reshape(1, dh).astype(F32), w_glu)
    return out.reshape(r * big_l, dh)


CONV_ROWS = 64
CONV_COLS = 256


def _conv_kernel(a_ref, g_ref, w_ref, b_ref, lng_ref, lnb_ref, o_ref, h_s, sh_s, y_s, *, rows):
    i = pl.program_id(1)

    @pl.when(i == 0)
    def _():
        h_s[0:CONV_HALO, :] = jnp.zeros((CONV_HALO, h_s.shape[1]), F32)

    @pl.when(i > 0)
    def _():
        h_s[0:CONV_HALO, :] = h_s[rows:rows + CONV_HALO, :]

    h_s[CONV_HALO:CONV_HALO + rows, :] = a_ref[...].astype(F32) * _sigmoid(g_ref[...].astype(F32))
    first_tap = CONV_HALO - (CONV_WIDTH - 1)
    shifted_rows = sh_s.shape[1]

    def col_block(cb, carry):
        c0 = pl.multiple_of(cb * CONV_COLS, CONV_COLS)
        cols = pl.ds(c0, CONV_COLS)
        for k in range(1, SUBLANES):
            sh_s[k - 1, :, cols] = h_s[k:k + shifted_rows, cols]
        for r0 in range(0, rows, CONV_ROWS):
            acc = jnp.zeros((CONV_ROWS, CONV_COLS), F32)
            for j in range(CONV_WIDTH):
                k = (first_tap + j) % SUBLANES
                base = r0 + first_tap + j - k
                window = (h_s[base:base + CONV_ROWS, cols] if k == 0
                          else sh_s[k - 1, base:base + CONV_ROWS, cols])
                acc = acc + w_ref[j:j + 1, cols] * window
            y_s[r0:r0 + CONV_ROWS, cols] = acc + b_ref[:, cols]
        return carry

    lax.fori_loop(0, h_s.shape[1] // CONV_COLS, col_block, 0)
    y = y_s[...]
    mu = jnp.mean(y, axis=-1, keepdims=True)
    yc = y - mu
    var = jnp.mean(yc * yc, axis=-1, keepdims=True)
    yn = yc * lax.rsqrt(var + LN_EPS) * lng_ref[...] + lnb_ref[...]
    o_ref[...] = (yn * _sigmoid(yn)).astype(o_ref.dtype)


def conv_module(z3, w, bias, ln_g, ln_b, *, rows=256):
    b, s, n = z3.shape
    dh = n // 6
    kern = functools.partial(_conv_kernel, rows=rows)
    vec = lambda v: v.reshape(1, dh).astype(F32)
    vec_spec = pl.BlockSpec((1, dh), lambda bi, i: (0, 0))
    return pl.pallas_call(
        kern,
        out_shape=jax.ShapeDtypeStruct((b, s, dh), BF16),
        grid=(b, s // rows),
        in_specs=[pl.BlockSpec((None, rows, dh), lambda bi, i: (bi, i, 0)),
                  pl.BlockSpec((None, rows, dh), lambda bi, i: (bi, i, 1)),
                  pl.BlockSpec((CONV_WIDTH, dh), lambda bi, i: (0, 0)), vec_spec, vec_spec, vec_spec],
        out_specs=pl.BlockSpec((None, rows, dh), lambda bi, i: (bi, i, 0)),
        scratch_shapes=[pltpu.VMEM((CONV_HALO + rows, dh), F32),
                        pltpu.VMEM((SUBLANES - 1, CONV_HALO + rows - SUBLANES, dh), F32),
                        pltpu.VMEM((rows, dh), F32)],
        compiler_params=_params("arbitrary", "arbitrary"),
        name="conv_module",
    )(z3, z3, w.astype(F32), vec(bias), vec(ln_g), vec(ln_b))


H_HEADS_PER_STEP = 4


def _split3(x):
    hi = x.astype(BF16)
    r1 = x - hi.astype(F32)
    mid = r1.astype(BF16)
    lo = (r1 - mid.astype(F32)).astype(BF16)
    return hi, mid, lo


def _hgrn_kernel(q_ref, f_ref, i_ref, g_ref, lb_ref, ng_ref, o_ref, st_s, *, rows):
    @pl.when(pl.program_id(2) == 0)
    def _():
        st_s[...] = jnp.zeros(st_s.shape, F32)

    big_l = H_CHUNK
    dk = H_EXPAND
    n_chunks = rows // big_l
    r_i = lax.broadcasted_iota(jnp.int32, (rows, rows), 0)
    c_i = lax.broadcasted_iota(jnp.int32, (rows, rows), 1)
    causal = jnp.logical_and(c_i <= r_i, r_i - c_i <= r_i % big_l)
    tri = jnp.where(causal, 1.0, 0.0).astype(BF16)

    def per_chunk_rows(x, row):
        return jnp.concatenate([jnp.broadcast_to(x[c * big_l + row:c * big_l + row + 1, :], (big_l, dk))
                                for c in range(n_chunks)], axis=0)

    for hh in range(st_s.shape[0]):
        cs = slice(hh * dk, (hh + 1) * dk)
        lb = lb_ref[:, cs]
        q = q_ref[:, cs].astype(F32)
        q = q * _sigmoid(q)
        fg = lb + (1.0 - lb) * _sigmoid(f_ref[:, cs].astype(F32))
        log_f = jnp.log(fg)
        k = 1.0 - fg
        v = i_ref[:, cs]
        cum3 = _dot(tri, jnp.concatenate(_split3(log_f), axis=1))
        bcum = cum3[:, :dk] + cum3[:, dk:2 * dk] + cum3[:, 2 * dk:]
        b_mid = per_chunk_rows(bcum, big_l // 2 - 1)
        b_last = per_chunk_rows(bcum, big_l - 1)
        q_in = (q * jnp.exp(bcum - b_mid)).astype(BF16)
        k_in = (k * jnp.exp(b_mid - bcum)).astype(BF16)
        scores = jnp.where(causal, _dot_nt(q_in, k_in), 0.0)
        o_intra = _dot(scores.astype(BF16), v)
        q_dec = (q * jnp.exp(bcum)).astype(BF16)
        k_dec = (k * jnp.exp(b_last - bcum)).astype(BF16)
        st = st_s[hh]
        o_inter = []
        for c in range(n_chunks):
            sl = slice(c * big_l, (c + 1) * big_l)
            o_inter.append(_dot_nt(q_dec[sl], st.astype(BF16)))
            st = st * jnp.exp(bcum[(c + 1) * big_l - 1:(c + 1) * big_l, :]) + _dot_tn(v[sl], k_dec[sl])
        st_s[hh] = st
        o = o_intra + jnp.concatenate(o_inter, axis=0)
        gate = g_ref[:, cs].astype(F32)
        o_ref[:, cs] = (_rms(o, ng_ref[...]) * (gate * _sigmoid(gate))).astype(o_ref.dtype)


def hgrn2(z3, lb, norm_g, *, rows=256):
    b, s, n = z3.shape
    dh = n // 6
    hb = min(H_HEADS_PER_STEP, dh // H_EXPAND)
    width = hb * H_EXPAND
    head_blocks = dh // width
    kern = functools.partial(_hgrn_kernel, rows=rows)

    def col(group):
        return pl.BlockSpec((None, rows, width), lambda bi, h, i: (bi, i, group * head_blocks + h))

    return pl.pallas_call(
        kern,
        out_shape=jax.ShapeDtypeStruct((b, s, dh), BF16),
        grid=(b, head_blocks, s // rows),
        in_specs=[col(2), col(3), col(4), col(5),
                  pl.BlockSpec((1, width), lambda bi, h, i: (0, h)),
                  pl.BlockSpec((1, H_EXPAND), lambda bi, h, i: (0, 0))],
        out_specs=pl.BlockSpec((None, rows, width), lambda bi, h, i: (bi, i, h)),
        scratch_shapes=[pltpu.VMEM((hb, H_EXPAND, H_EXPAND), F32)],
        compiler_params=_params("parallel", "parallel", "arbitrary"),
        name="hgrn2",
    )(z3, z3, z3, z3, lb.reshape(1, dh).astype(F32), norm_g.reshape(1, H_EXPAND).astype(F32))


def _tile(n, want):
    if n <= want:
        return n
    t = want - want % LANES
    while n % t:
        t -= LANES
    return t


def kernel(x, norm_gains, ffn_w_gate, ffn_w_up, ffn_w_down, ev_w_in, ev_w_out, diff_lambda, diff_subln, s5_lambda_re, s5_lambda_im, s5_log_step, s5_B_re, s5_B_im, s5_C_re, s5_C_im, s5_D, s5_w_glu, od_w_in, od_w_out, conv_w, conv_b, conv_ln_g, conv_ln_b, hgrn_lb, hgrn_norm_g):
    bsz, seq, d = x.shape
    depth = norm_gains.shape[0]
    t = bsz * seq
    dh = d // 2
    d_ff = ffn_w_gate.shape[-1]
    tm = _tile(t, 1024)
    attn_blk = _tile(seq, 512)
    n_chunks = seq // S5_CHUNK
    tn_w = _tile(dh, 512)

    lb_all = jnp.cumsum(jax.nn.softmax(hgrn_lb.astype(F32), axis=0), axis=0)
    lb_all = lb_all - lb_all[0]

    xf = x.reshape(t, d)
    h = rmsnorm(xf, norm_gains[0, 0])
    for l in range(depth):
        gains = norm_gains[l]
        if l % 2 == 0:
            e = l // 2
            lam_init = 0.8 - 0.6 * math.exp(-0.3 * l)
            qkv = matmul(h, ev_w_in, tm=tm, tn=tn_w, layer=e, n=3 * dh)
            u_cm = matmul_chunk_major(h, ev_w_in, tn=tn_w, n=dh, layer=e, col_start=3 * dh // tn_w)
            lf = diff_lambda[e].astype(F32)
            lam = jnp.exp(jnp.sum(lf[0] * lf[1])) - jnp.exp(jnp.sum(lf[2] * lf[3])) + lam_init
            o_a = diff_attention(qkv.reshape(bsz, seq, 3 * dh), lam, diff_subln[e], lam_init,
                                 blk=attn_blk).reshape(t, dh)
            tables = _s5_tables(s5_lambda_re[e], s5_lambda_im[e], s5_log_step[e], s5_B_re[e], s5_B_im[e],
                                s5_C_re[e], s5_C_im[e], n_chunks)
            y_cm = s5_scan(u_cm, tables, n_chunks)
            o_b = s5_glu(y_cm, u_cm, s5_D[e], s5_w_glu[e].astype(BF16), rows=_tile(t // S5_CHUNK, 512))
            mix = matmul_cat(o_a, o_b, ev_w_out, tm=tm, tn=tn_w, layer=e)
        else:
            o = l // 2
            z = matmul(h, od_w_in, tm=tm, tn=tn_w, layer=o)
            z3 = z.reshape(bsz, seq, 6 * dh)
            o_c = conv_module(z3, conv_w[o], conv_b[o], conv_ln_g[o], conv_ln_b[o]).reshape(t, dh)
            o_d = hgrn2(z3, lb_all[l], hgrn_norm_g[o]).reshape(t, dh)
            mix = matmul_cat(o_c, o_d, od_w_out, tm=tm, tn=tn_w, layer=o)
        xf, h = resnorm(mix, xf, gains[1], gains[2])
        hidden = swiglu_gate_up(h, ffn_w_gate, ffn_w_up, tm=tm, tn=_tile(d_ff, 256), layer=l)
        down = matmul(hidden, ffn_w_down[l].astype(BF16), tm=_tile(t, 512), tn=_tile(d, 512))
        xf, h = resnorm(down, xf, gains[3], norm_gains[l + 1, 0] if l + 1 < depth else None)
    return xf.reshape(bsz, seq, d)
```

```python
import functools
import math

import jax
import jax.numpy as jnp
from jax import lax
from jax.experimental import pallas as pl
from jax.experimental.pallas import tpu as pltpu

F32 = jnp.float32
BF16 = jnp.bfloat16

A_HEAD_DIM = 64
S5_GROUP = 16
S5_STATE = 64
S5_CHUNK = 16
CONV_WIDTH = 31
CONV_HALO = 32
H_EXPAND = 128
H_CHUNK = 64
RMS_EPS = 1e-6
LN_EPS = 1e-5
LANES = 128
SUBLANES = 8
VMEM_LIMIT_BYTES = 56 * 1024 * 1024
NEG_BIG = -1e30
S5_BLOCK_GROUPS = LANES // S5_GROUP
S5_BLOCK_STATE = S5_BLOCK_GROUPS * S5_STATE


def _params(*sem):
    return pltpu.CompilerParams(dimension_semantics=sem, vmem_limit_bytes=VMEM_LIMIT_BYTES)


def _sigmoid(x):
    return 1.0 / (1.0 + jnp.exp(-x))


def _dot(a, b):
    return jnp.dot(a, b, preferred_element_type=F32)


def _dot_nt(a, b):
    return lax.dot_general(a, b, (((1,), (1,)), ((), ())), preferred_element_type=F32)


def _dot_tn(a, b):
    return lax.dot_general(a, b, (((0,), (0,)), ((), ())), preferred_element_type=F32)


def _rms(x, g):
    return x * lax.rsqrt(jnp.mean(x * x, axis=-1, keepdims=True) + RMS_EPS) * g


def _rmsnorm_kernel(x_ref, g_ref, o_ref):
    o_ref[...] = _rms(x_ref[...].astype(F32), g_ref[...]).astype(o_ref.dtype)


def rmsnorm(x, g, *, rows=512):
    t, d = x.shape
    return pl.pallas_call(
        _rmsnorm_kernel,
        out_shape=jax.ShapeDtypeStruct((t, d), BF16),
        grid=(t // rows,),
        in_specs=[pl.BlockSpec((rows, d), lambda i: (i, 0)), pl.BlockSpec((1, d), lambda i: (0, 0))],
        out_specs=pl.BlockSpec((rows, d), lambda i: (i, 0)),
        compiler_params=_params("parallel"),
        name="rmsnorm",
    )(x, g.reshape(1, d).astype(F32))


def _resnorm_kernel(y_ref, x_ref, gpost_ref, gpre_ref, xo_ref, ho_ref):
    xn = x_ref[...] + _rms(y_ref[...].astype(F32), gpost_ref[...])
    xo_ref[...] = xn
    ho_ref[...] = _rms(xn, gpre_ref[...]).astype(ho_ref.dtype)


def _resnorm_last_kernel(y_ref, x_ref, gpost_ref, xo_ref):
    xo_ref[...] = x_ref[...] + _rms(y_ref[...].astype(F32), gpost_ref[...])


def resnorm(y, x, g_post, g_pre, *, rows=256):
    t, d = x.shape
    row_spec = pl.BlockSpec((rows, d), lambda i: (i, 0))
    gain_spec = pl.BlockSpec((1, d), lambda i: (0, 0))
    if g_pre is None:
        return pl.pallas_call(
            _resnorm_last_kernel,
            out_shape=jax.ShapeDtypeStruct((t, d), F32),
            grid=(t // rows,),
            in_specs=[row_spec, row_spec, gain_spec],
            out_specs=row_spec,
            compiler_params=_params("parallel"),
            name="resnorm_last",
        )(y, x, g_post.reshape(1, d).astype(F32)), None
    return pl.pallas_call(
        _resnorm_kernel,
        out_shape=(jax.ShapeDtypeStruct((t, d), F32), jax.ShapeDtypeStruct((t, d), BF16)),
        grid=(t // rows,),
        in_specs=[row_spec, row_spec, gain_spec, gain_spec],
        out_specs=(row_spec, row_spec),
        compiler_params=_params("parallel"),
        name="resnorm",
    )(y, x, g_post.reshape(1, d).astype(F32), g_pre.reshape(1, d).astype(F32))


def _mm_kernel(x_ref, w_ref, o_ref):
    o_ref[...] = _dot(x_ref[...], w_ref[...].astype(BF16)).astype(o_ref.dtype)


def _weight_spec(w, k, tn, layer, col_start, row_block=0):
    if w.ndim == 2:
        return pl.BlockSpec((k, tn), lambda i, j: (row_block, j + col_start))
    return pl.BlockSpec((None, k, tn), lambda i, j: (layer, row_block, j + col_start))


def matmul(x, w, *, tm, tn, out_dtype=BF16, layer=0, col_start=0, n=None):
    m, k = x.shape
    n = w.shape[-1] if n is None else n
    return pl.pallas_call(
        _mm_kernel,
        out_shape=jax.ShapeDtypeStruct((m, n), out_dtype),
        grid=(m // tm, n // tn),
        in_specs=[pl.BlockSpec((tm, k), lambda i, j: (i, 0)), _weight_spec(w, k, tn, layer, col_start)],
        out_specs=pl.BlockSpec((tm, tn), lambda i, j: (i, j)),
        compiler_params=_params("parallel", "arbitrary"),
        name="matmul",
    )(x, w)


def _mm_chunk_major_kernel(x_ref, w_ref, o_ref):
    acc = _dot(x_ref[...], w_ref[...].astype(BF16))
    for p in range(o_ref.shape[0]):
        o_ref[p] = acc[:, p * LANES:(p + 1) * LANES].astype(o_ref.dtype)


def matmul_chunk_major(x, w, *, tn, n, layer=0, col_start=0):
    t, k = x.shape
    big_l = S5_CHUNK
    r = t // big_l
    return pl.pallas_call(
        _mm_chunk_major_kernel,
        out_shape=jax.ShapeDtypeStruct((big_l, n // LANES, r, LANES), BF16),
        grid=(big_l, n // tn),
        in_specs=[pl.BlockSpec((r, k), lambda l, j: (0, l)), _weight_spec(w, k, tn, layer, col_start)],
        out_specs=pl.BlockSpec((None, tn // LANES, r, LANES), lambda l, j: (l, j, 0, 0)),
        compiler_params=_params("parallel", "arbitrary"),
        name="matmul_chunk_major",
    )(x.reshape(r, big_l * k), w)


def _mm2_kernel(x1_ref, x2_ref, w1_ref, w2_ref, o_ref):
    o_ref[...] = (_dot(x1_ref[...], w1_ref[...].astype(BF16))
                  + _dot(x2_ref[...], w2_ref[...].astype(BF16))).astype(o_ref.dtype)


def matmul_cat(x1, x2, w, *, tm, tn, out_dtype=BF16, layer=0):
    m, kh = x1.shape
    n = w.shape[-1]
    return pl.pallas_call(
        _mm2_kernel,
        out_shape=jax.ShapeDtypeStruct((m, n), out_dtype),
        grid=(m // tm, n // tn),
        in_specs=[pl.BlockSpec((tm, kh), lambda i, j: (i, 0)), pl.BlockSpec((tm, kh), lambda i, j: (i, 0)),
                  _weight_spec(w, kh, tn, layer, 0, row_block=0), _weight_spec(w, kh, tn, layer, 0, row_block=1)],
        out_specs=pl.BlockSpec((tm, tn), lambda i, j: (i, j)),
        compiler_params=_params("parallel", "arbitrary"),
        name="matmul_cat",
    )(x1, x2, w, w)


def _gateup_kernel(x_ref, wg_ref, wu_ref, o_ref):
    x = x_ref[...]
    g = _dot(x, wg_ref[...].astype(BF16))
    u = _dot(x, wu_ref[...].astype(BF16))
    o_ref[...] = (g * _sigmoid(g) * u).astype(o_ref.dtype)


def swiglu_gate_up(x, wg, wu, *, tm, tn, layer=0):
    m, k = x.shape
    n = wg.shape[-1]
    w_spec = _weight_spec(wg, k, tn, layer, 0)
    return pl.pallas_call(
        _gateup_kernel,
        out_shape=jax.ShapeDtypeStruct((m, n), BF16),
        grid=(m // tm, n // tn),
        in_specs=[pl.BlockSpec((tm, k), lambda i, j: (i, 0)), w_spec, w_spec],
        out_specs=pl.BlockSpec((tm, tn), lambda i, j: (i, j)),
        compiler_params=_params("parallel", "arbitrary"),
        name="swiglu_gate_up",
    )(x, wg, wu)


def _diffattn_kernel(lam_ref, q_ref, k_ref, v_ref, g_ref, o_ref, q2_s, vx_s, m_s, acc_s, s_s, *, blk, lam_init):
    i = pl.program_id(2)

    @pl.when(i == 0)
    def _():
        vx_s[:, :LANES] = v_ref[...]
        vx_s[:, LANES:] = jnp.ones((vx_s.shape[0], LANES), vx_s.dtype)

    q = q_ref[...]
    lane = lax.broadcasted_iota(jnp.int32, q.shape, 1)
    qs = q * (A_HEAD_DIM ** -0.5)
    zero = jnp.zeros_like(qs)
    q2_s[:blk, :] = jnp.where(lane < A_HEAD_DIM, qs, zero)
    q2_s[blk:, :] = jnp.where(lane >= A_HEAD_DIM, qs, zero)
    m_s[...] = jnp.full(m_s.shape, NEG_BIG, F32)
    acc_s[...] = jnp.zeros(acc_s.shape, F32)

    def scores(j):
        return _dot_nt(q2_s[...], k_ref[pl.ds(pl.multiple_of(j * blk, blk), blk), :])

    s_s[0] = scores(0)

    def step(j, s, diagonal):
        start = pl.multiple_of(j * blk, blk)
        if diagonal:
            row = lax.broadcasted_iota(jnp.int32, s.shape, 0)
            row = jnp.where(row >= blk, row - blk, row)
            col = lax.broadcasted_iota(jnp.int32, s.shape, 1)
            s = jnp.where(col <= row, s, NEG_BIG)
        m_prev = m_s[...]
        m_new = jnp.maximum(m_prev, jnp.max(s, axis=-1, keepdims=True))
        alpha = jnp.exp(m_prev - m_new)
        p = jnp.concatenate([jnp.exp(s[:, c:c + LANES] - m_new) for c in range(0, blk, LANES)], axis=1)
        pv = _dot(p.astype(BF16), vx_s[pl.ds(start, blk), :])
        acc_s[...] = jnp.concatenate([alpha, alpha], axis=1) * acc_s[...] + pv
        m_s[...] = m_new

    def pair(jj, carry):
        j = 2 * jj
        s_s[1] = scores(j + 1)
        step(j, s_s[0], False)
        s_s[0] = scores(j + 2)
        step(j + 1, s_s[1], False)
        return carry

    lax.fori_loop(0, i // 2, pair, 0)

    @pl.when(i % 2 == 1)
    def _():
        s_s[1] = scores(i)
        step(i - 1, s_s[0], False)

    step(i, s_s[i % 2], True)

    acc = acc_s[...]
    o = acc[:blk, :LANES] / acc[:blk, LANES:] - lam_ref[0] * (acc[blk:, :LANES] / acc[blk:, LANES:])
    o_ref[...] = (_rms(o, g_ref[...]) * (1.0 - lam_init)).astype(o_ref.dtype)


def diff_attention(z3, lam, subln_g, lam_init, *, blk):
    b, s, n = z3.shape
    heads = n // 3 // LANES
    kern = functools.partial(_diffattn_kernel, blk=blk, lam_init=lam_init)
    return pl.pallas_call(
        kern,
        out_shape=jax.ShapeDtypeStruct((b, s, heads * LANES), BF16),
        grid=(b, heads, s // blk),
        in_specs=[
            pl.BlockSpec(memory_space=pltpu.SMEM),
            pl.BlockSpec((None, blk, LANES), lambda bi, h, i: (bi, i, h)),
            pl.BlockSpec((None, s, LANES), lambda bi, h, i: (bi, 0, heads + h)),
            pl.BlockSpec((None, s, LANES), lambda bi, h, i: (bi, 0, 2 * heads + h)),
            pl.BlockSpec((1, LANES), lambda bi, h, i: (0, 0)),
        ],
        out_specs=pl.BlockSpec((None, blk, LANES), lambda bi, h, i: (bi, i, h)),
        scratch_shapes=[pltpu.VMEM((2 * blk, LANES), BF16), pltpu.VMEM((s, 2 * LANES), BF16),
                        pltpu.VMEM((2 * blk, LANES), F32), pltpu.VMEM((2 * blk, 2 * LANES), F32),
                        pltpu.VMEM((2, 2 * blk, blk), F32)],
        compiler_params=_params("arbitrary", "arbitrary", "arbitrary"),
        name="diff_attention",
    )(lam.reshape(1).astype(F32), z3, z3, z3, subln_g.reshape(1, LANES).astype(F32))


def _s5_tables(lam_re, lam_im, log_step, b_re, b_im, c_re, c_im, n_chunks):
    hp = lax.Precision.HIGHEST
    g_cnt, n = lam_re.shape
    ch = b_re.shape[-1]
    big_l = S5_CHUNK
    gb = S5_BLOCK_GROUPS
    blocks = g_cnt // gb
    dt = jnp.exp(log_step.astype(F32))[:, None]
    lr, li = lam_re.astype(F32), lam_im.astype(F32)
    mag = jnp.exp(lr * dt)
    ang = li * dt
    a_re, a_im = mag * jnp.cos(ang), mag * jnp.sin(ang)
    den = lr * lr + li * li
    num_re, num_im = a_re - 1.0, a_im
    fz_re = (num_re * lr + num_im * li) / den
    fz_im = (num_im * lr - num_re * li) / den
    br, bi = b_re.astype(F32), b_im.astype(F32)
    bb_re = fz_re[..., None] * br - fz_im[..., None] * bi
    bb_im = fz_re[..., None] * bi + fz_im[..., None] * br

    pr, pi = [jnp.ones_like(a_re)], [jnp.zeros_like(a_im)]
    for _ in range(big_l):
        pr.append(pr[-1] * a_re - pi[-1] * a_im)
        pi.append(pr[-2] * a_im + pi[-1] * a_re)
    pw_re, pw_im = jnp.stack(pr), jnp.stack(pi)

    cr, ci = c_re.astype(F32), c_im.astype(F32)
    bbt_re, bbt_im = bb_re.transpose(0, 2, 1), bb_im.transpose(0, 2, 1)

    p_re = pw_re[:big_l, :, None, :] * bbt_re[None] - pw_im[:big_l, :, None, :] * bbt_im[None]
    p_im = pw_re[:big_l, :, None, :] * bbt_im[None] + pw_im[:big_l, :, None, :] * bbt_re[None]
    k_lag = (jnp.einsum('gcn,jgdn->jgdc', cr, p_re, precision=hp)
             - jnp.einsum('gcn,jgdn->jgdc', ci, p_im, precision=hp))
    k_bd = (k_lag.reshape(big_l, blocks, gb, ch, 1, ch) * jnp.eye(gb, dtype=F32)[:, None, :, None])
    k_bd = k_bd.reshape(big_l, blocks, gb * ch, gb * ch)
    lag_tab = jnp.concatenate([k_bd[::-1].transpose(1, 0, 2, 3).reshape(blocks, big_l * gb * ch, gb * ch),
                               jnp.zeros((blocks, gb * ch, gb * ch), F32)], axis=1)

    n2 = 2 * n
    tile_n = jnp.tile(jnp.eye(n, dtype=F32), (1, 2))
    spread_g = jnp.repeat(jnp.eye(gb, dtype=F32), ch, axis=1)
    keep_in = (jnp.arange(gb)[:, None] % 2 == (jnp.arange(n2) // n)[None, :]).astype(F32)
    keep_out = (jnp.arange(2)[:, None] == ((jnp.arange(gb * ch) // ch) % 2)[None, :]).astype(F32)

    def tile2(x):
        return jnp.dot(x, tile_n, precision=hp)

    pwr = tile2(pw_re[:big_l][::-1]).reshape(big_l, blocks, gb, 1, n2)
    pwi = tile2(pw_im[:big_l][::-1]).reshape(big_l, blocks, gb, 1, n2)
    bbr = tile2(bbt_re).reshape(1, blocks, gb, ch, n2) * keep_in[:, None, :]
    bbi = tile2(bbt_im).reshape(1, blocks, gb, ch, n2) * keep_in[:, None, :]

    def rows_in(w):
        return w.transpose(1, 0, 2, 3, 4).reshape(blocks, big_l * gb * ch, n2)

    w_in = jnp.concatenate([rows_in(pwr * bbr - pwi * bbi), rows_in(pwr * bbi + pwi * bbr)], axis=2)

    def spread_pln(x):
        return jnp.dot(x.reshape(big_l, blocks, gb, n).transpose(1, 0, 3, 2), spread_g, precision=hp)

    def c_pn(x):
        return x.reshape(blocks, gb, ch, n).transpose(0, 3, 1, 2).reshape(blocks, 1, n, gb * ch)

    art, ait = spread_pln(pw_re[1:]), spread_pln(pw_im[1:])
    crp, cip = c_pn(cr), c_pn(ci)

    def rows_out(w):
        return (w[:, :, None] * keep_out[:, None, :]).reshape(blocks, big_l, n2, gb * ch)

    w_out = jnp.concatenate([rows_out(crp * art - cip * ait), rows_out(-(crp * ait + cip * art))], axis=2)

    sr, si = [pw_re[big_l]], [pw_im[big_l]]
    steps = max(1, int(math.ceil(math.log2(n_chunks))))
    for _ in range(steps - 1):
        sr.append(sr[-1] * sr[-1] - si[-1] * si[-1])
        si.append(2.0 * sr[-2] * si[-1])
    scan_pw = jnp.stack(sr + si, axis=1).reshape(blocks, gb, 2 * steps, n)
    scan_pw = scan_pw.transpose(0, 2, 1, 3).reshape(blocks, 2 * steps, gb * n)
    return lag_tab.astype(BF16), w_in.astype(BF16), w_out.astype(BF16), scan_pw, steps


def _s5_kernel(u_ref, lag_ref, win_ref, wout_ref, pw_ref, y_ref, win_s, wout_s, *, steps):
    big_l = u_ref.shape[0]
    pairs = S5_BLOCK_GROUPS // 2
    pair_width = 2 * S5_GROUP

    @pl.when(pl.program_id(1) == 0)
    def _():
        row_pair = (lax.broadcasted_iota(jnp.int32, (win_ref.shape[0], LANES), 0) // pair_width) % pairs
        col_pair = (lax.broadcasted_iota(jnp.int32, (LANES, LANES), 1) // pair_width) % pairs
        for k in range(pairs):
            keep_rows = jnp.where(row_pair == k, 1.0, 0.0).astype(BF16)
            keep_cols = jnp.where(col_pair == k, 1.0, 0.0).astype(BF16)
            for half in range(2):
                src = slice(half * LANES, (half + 1) * LANES)
                dst = slice((half * pairs + k) * LANES, (half * pairs + k + 1) * LANES)
                win_s[:, dst] = win_ref[:, src] * keep_rows
                for l in range(big_l):
                    wout_s[dst, l * LANES:(l + 1) * LANES] = wout_ref[l, src, :] * keep_cols

    ucat = jnp.concatenate([u_ref[l] for l in range(big_l)], axis=1)
    inj = _dot(ucat, win_s[...])
    nb = inj.shape[1] // 2
    xr, xi = inj[:, :nb], inj[:, nb:]
    chunk = lax.broadcasted_iota(jnp.int32, xr.shape, 0)
    for t in range(steps):
        d = 1 << t
        ar = pw_ref[t:t + 1, :]
        ai = pw_ref[steps + t:steps + t + 1, :]
        sr = pltpu.roll(xr, d, 0)
        si = pltpu.roll(xi, d, 0)
        ok = chunk >= d
        xr, xi = (xr + jnp.where(ok, ar * sr - ai * si, 0.0),
                  xi + jnp.where(ok, ar * si + ai * sr, 0.0))
    first = chunk == 0
    xsr = jnp.where(first, 0.0, pltpu.roll(xr, 1, 0))
    xsi = jnp.where(first, 0.0, pltpu.roll(xi, 1, 0))
    y_state = _dot(jnp.concatenate([xsr, xsi], axis=1).astype(BF16), wout_s[...])
    for m in range(big_l // 2):
        even = lag_ref[(big_l - 1 - 2 * m) * LANES:(big_l + 1) * LANES, :]
        odd = lag_ref[(big_l - 2 - 2 * m) * LANES:big_l * LANES, :]
        w = jnp.concatenate([even, odd], axis=1)
        y2 = _dot(ucat[:, :(2 * m + 2) * LANES], w) + y_state[:, 2 * m * LANES:(2 * m + 2) * LANES]
        y_ref[2 * m] = y2[:, :LANES].astype(y_ref.dtype)
        y_ref[2 * m + 1] = y2[:, LANES:].astype(y_ref.dtype)


def s5_scan(u_cm, tables, n_chunks):
    lag_tab, w_in, w_out, scan_pw, steps = tables
    big_l, blocks, rows, _ = u_cm.shape
    nb2 = 2 * S5_BLOCK_STATE
    kern = functools.partial(_s5_kernel, steps=steps)
    plane = pl.BlockSpec((big_l, None, n_chunks, LANES), lambda p, b: (0, p, b, 0))
    return pl.pallas_call(
        kern,
        out_shape=jax.ShapeDtypeStruct(u_cm.shape, BF16),
        grid=(blocks, rows // n_chunks),
        in_specs=[
            plane,
            pl.BlockSpec((None, (big_l + 1) * LANES, LANES), lambda p, b: (p, 0, 0)),
            pl.BlockSpec((None, big_l * LANES, 2 * LANES), lambda p, b: (p, 0, 0)),
            pl.BlockSpec((None, big_l, 2 * LANES, LANES), lambda p, b: (p, 0, 0, 0)),
            pl.BlockSpec((None, 2 * steps, S5_BLOCK_STATE), lambda p, b: (p, 0, 0)),
        ],
        out_specs=plane,
        scratch_shapes=[pltpu.VMEM((big_l * LANES, nb2), BF16), pltpu.VMEM((nb2, big_l * LANES), BF16)],
        compiler_params=_params("parallel", "arbitrary"),
        name="s5_scan",
    )(u_cm, lag_tab, w_in, w_out, scan_pw)


def _s5_glu_kernel(y_ref, u_ref, d_ref, w_ref, o_ref):
    blocks = y_ref.shape[0]
    y = jnp.concatenate([y_ref[p] for p in range(blocks)], axis=1).astype(F32)
    u = jnp.concatenate([u_ref[p] for p in range(blocks)], axis=1).astype(F32)
    y = y + d_ref[...] * u
    z = 0.5 * y * (1.0 + jnp.tanh(math.sqrt(2.0 / math.pi) * (y + 0.044715 * (y * y * y))))
    gate = _dot(z.astype(BF16), w_ref[...])
    o_ref[...] = (z * _sigmoid(gate)).astype(o_ref.dtype)


def s5_glu(y_cm, u_cm, d_skip, w_glu, *, rows=512):
    big_l, blocks, r, _ = y_cm.shape
    dh = blocks * LANES
    plane = pl.BlockSpec((None, blocks, rows, LANES), lambda l, i: (l, 0, i, 0))
    out = pl.pallas_call(
        _s5_glu_kernel,
        out_shape=jax.ShapeDtypeStruct((r, big_l * dh), BF16),
        grid=(big_l, r // rows),
        in_specs=[plane, plane, pl.BlockSpec((1, dh), lambda l, i: (0, 0)),
                  pl.BlockSpec((dh, dh), lambda l, i: (0, 0))],
        out_specs=pl.BlockSpec((rows, dh), lambda l, i: (i, l)),
        compiler_params=_params("parallel", "parallel"),
        name="s5_glu",
    )(y_cm, u_cm, d_skip.reshape(1, dh).astype(F32), w_glu)
    return out.reshape(r * big_l, dh)


CONV_ROWS = 64
CONV_COLS = 256


def _conv_kernel(a_ref, g_ref, w_ref, b_ref, lng_ref, lnb_ref, o_ref, h_s, sh_s, y_s, *, rows):
    i = pl.program_id(1)

    @pl.when(i == 0)
    def _():
        h_s[0:CONV_HALO, :] = jnp.zeros((CONV_HALO, h_s.shape[1]), F32)

    @pl.when(i > 0)
    def _():
        h_s[0:CONV_HALO, :] = h_s[rows:rows + CONV_HALO, :]

    h_s[CONV_HALO:CONV_HALO + rows, :] = a_ref[...].astype(F32) * _sigmoid(g_ref[...].astype(F32))
    first_tap = CONV_HALO - (CONV_WIDTH - 1)
    shifted_rows = sh_s.shape[1]

    def col_block(cb, carry):
        c0 = pl.multiple_of(cb * CONV_COLS, CONV_COLS)
        cols = pl.ds(c0, CONV_COLS)
        for k in range(1, SUBLANES):
            sh_s[k - 1, :, cols] = h_s[k:k + shifted_rows, cols]
        for r0 in range(0, rows, CONV_ROWS):
            acc = jnp.zeros((CONV_ROWS, CONV_COLS), F32)
            for j in range(CONV_WIDTH):
                k = (first_tap + j) % SUBLANES
                base = r0 + first_tap + j - k
                window = (h_s[base:base + CONV_ROWS, cols] if k == 0
                          else sh_s[k - 1, base:base + CONV_ROWS, cols])
                acc = acc + w_ref[j:j + 1, cols] * window
            y_s[r0:r0 + CONV_ROWS, cols] = acc + b_ref[:, cols]
        return carry

    lax.fori_loop(0, h_s.shape[1] // CONV_COLS, col_block, 0)
    y = y_s[...]
    mu = jnp.mean(y, axis=-1, keepdims=True)
    yc = y - mu
    var = jnp.mean(yc * yc, axis=-1, keepdims=True)
    yn = yc * lax.rsqrt(var + LN_EPS) * lng_ref[...] + lnb_ref[...]
    o_ref[...] = (yn * _sigmoid(yn)).astype(o_ref.dtype)


def conv_module(z3, w, bias, ln_g, ln_b, *, rows=256):
    b, s, n = z3.shape
    dh = n // 6
    kern = functools.partial(_conv_kernel, rows=rows)
    vec = lambda v: v.reshape(1, dh).astype(F32)
    vec_spec = pl.BlockSpec((1, dh), lambda bi, i: (0, 0))
    return pl.pallas_call(
        kern,
        out_shape=jax.ShapeDtypeStruct((b, s, dh), BF16),
        grid=(b, s // rows),
        in_specs=[pl.BlockSpec((None, rows, dh), lambda bi, i: (bi, i, 0)),
                  pl.BlockSpec((None, rows, dh), lambda bi, i: (bi, i, 1)),
                  pl.BlockSpec((CONV_WIDTH, dh), lambda bi, i: (0, 0)), vec_spec, vec_spec, vec_spec],
        out_specs=pl.BlockSpec((None, rows, dh), lambda bi, i: (bi, i, 0)),
        scratch_shapes=[pltpu.VMEM((CONV_HALO + rows, dh), F32),
                        pltpu.VMEM((SUBLANES - 1, CONV_HALO + rows - SUBLANES, dh), F32),
                        pltpu.VMEM((rows, dh), F32)],
        compiler_params=_params("arbitrary", "arbitrary"),
        name="conv_module",
    )(z3, z3, w.astype(F32), vec(bias), vec(ln_g), vec(ln_b))


H_HEADS_PER_STEP = 4


def _split3(x):
    hi = x.astype(BF16)
    r1 = x - hi.astype(F32)
    mid = r1.astype(BF16)
    lo = (r1 - mid.astype(F32)).astype(BF16)
    return hi, mid, lo


def _hgrn_kernel(q_ref, f_ref, i_ref, g_ref, lb_ref, ng_ref, o_ref, st_s, *, rows):
    @pl.when(pl.program_id(2) == 0)
    def _():
        st_s[...] = jnp.zeros(st_s.shape, F32)

    big_l = H_CHUNK
    dk = H_EXPAND
    n_chunks = rows // big_l
    r_i = lax.broadcasted_iota(jnp.int32, (rows, rows), 0)
    c_i = lax.broadcasted_iota(jnp.int32, (rows, rows), 1)
    causal = jnp.logical_and(c_i <= r_i, r_i - c_i <= r_i % big_l)
    tri = jnp.where(causal, 1.0, 0.0).astype(BF16)

    def per_chunk_rows(x, row):
        return jnp.concatenate([jnp.broadcast_to(x[c * big_l + row:c * big_l + row + 1, :], (big_l, dk))
                                for c in range(n_chunks)], axis=0)

    for hh in range(st_s.shape[0]):
        cs = slice(hh * dk, (hh + 1) * dk)
        lb = lb_ref[:, cs]
        q = q_ref[:, cs].astype(F32)
        q = q * _sigmoid(q)
        fg = lb + (1.0 - lb) * _sigmoid(f_ref[:, cs].astype(F32))
        log_f = jnp.log(fg)
        k = 1.0 - fg
        v = i_ref[:, cs]
        cum3 = _dot(tri, jnp.concatenate(_split3(log_f), axis=1))
        bcum = cum3[:, :dk] + cum3[:, dk:2 * dk] + cum3[:, 2 * dk:]
        b_mid = per_chunk_rows(bcum, big_l // 2 - 1)
        b_last = per_chunk_rows(bcum, big_l - 1)
        q_in = (q * jnp.exp(bcum - b_mid)).astype(BF16)
        k_in = (k * jnp.exp(b_mid - bcum)).astype(BF16)
        scores = jnp.where(causal, _dot_nt(q_in, k_in), 0.0)
        o_intra = _dot(scores.astype(BF16), v)
        q_dec = (q * jnp.exp(bcum)).astype(BF16)
        k_dec = (k * jnp.exp(b_last - bcum)).astype(BF16)
        st = st_s[hh]
        o_inter = []
        for c in range(n_chunks):
            sl = slice(c * big_l, (c + 1) * big_l)
            o_inter.append(_dot_nt(q_dec[sl], st.astype(BF16)))
            st = st * jnp.exp(bcum[(c + 1) * big_l - 1:(c + 1) * big_l, :]) + _dot_tn(v[sl], k_dec[sl])
        st_s[hh] = st
        o = o_intra + jnp.concatenate(o_inter, axis=0)
        gate = g_ref[:, cs].astype(F32)
        o_ref[:, cs] = (_rms(o, ng_ref[...]) * (gate * _sigmoid(gate))).astype(o_ref.dtype)


def hgrn2(z3, lb, norm_g, *, rows=256):
    b, s, n = z3.shape
    dh = n // 6
    hb = min(H_HEADS_PER_STEP, dh // H_EXPAND)
    width = hb * H_EXPAND
    head_blocks = dh // width
    kern = functools.partial(_hgrn_kernel, rows=rows)

    def col(group):
        return pl.BlockSpec((None, rows, width), lambda bi, h, i: (bi, i, group * head_blocks + h))

    return pl.pallas_call(
        kern,
        out_shape=jax.ShapeDtypeStruct((b, s, dh), BF16),
        grid=(b, head_blocks, s // rows),
        in_specs=[col(2), col(3), col(4), col(5),
                  pl.BlockSpec((1, width), lambda bi, h, i: (0, h)),
                  pl.BlockSpec((1, H_EXPAND), lambda bi, h, i: (0, 0))],
        out_specs=pl.BlockSpec((None, rows, width), lambda bi, h, i: (bi, i, h)),
        scratch_shapes=[pltpu.VMEM((hb, H_EXPAND, H_EXPAND), F32)],
        compiler_params=_params("parallel", "parallel", "arbitrary"),
        name="hgrn2",
    )(z3, z3, z3, z3, lb.reshape(1, dh).astype(F32), norm_g.reshape(1, H_EXPAND).astype(F32))


def _tile(n, want):
    if n <= want:
        return n
    t = want - want % LANES
    while n % t:
        t -= LANES
    return t


def kernel(x, norm_gains, ffn_w_gate, ffn_w_up, ffn_w_down, ev_w_in, ev_w_out, diff_lambda, diff_subln, s5_lambda_re, s5_lambda_im, s5_log_step, s5_B_re, s5_B_im, s5_C_re, s5_C_im, s5_D, s5_w_glu, od_w_in, od_w_out, conv_w, conv_b, conv_ln_g, conv_ln_b, hgrn_lb, hgrn_norm_g):
    bsz, seq, d = x.shape
    depth = norm_gains.shape[0]
    t = bsz * seq
    dh = d // 2
    d_ff = ffn_w_gate.shape[-1]
    tm = _tile(t, 1024)
    attn_blk = _tile(seq, 512)
    n_chunks = seq // S5_CHUNK
    tn_w = _tile(dh, 512)

    lb_all = jnp.cumsum(jax.nn.softmax(hgrn_lb.astype(F32), axis=0), axis=0)
    lb_all = lb_all - lb_all[0]

    w_down = ffn_w_down.astype(BF16)

    xf = x.reshape(t, d)
    h = rmsnorm(xf, norm_gains[0, 0])
    for l in range(depth):
        gains = norm_gains[l]
        if l % 2 == 0:
            e = l // 2
            lam_init = 0.8 - 0.6 * math.exp(-0.3 * l)
            qkv = matmul(h, ev_w_in, tm=tm, tn=tn_w, layer=e, n=3 * dh)
            u_cm = matmul_chunk_major(h, ev_w_in, tn=tn_w, n=dh, layer=e, col_start=3 * dh // tn_w)
            lf = diff_lambda[e].astype(F32)
            lam = jnp.exp(jnp.sum(lf[0] * lf[1])) - jnp.exp(jnp.sum(lf[2] * lf[3])) + lam_init
            o_a = diff_attention(qkv.reshape(bsz, seq, 3 * dh), lam, diff_subln[e], lam_init,
                                 blk=attn_blk).reshape(t, dh)
            tables = _s5_tables(s5_lambda_re[e], s5_lambda_im[e], s5_log_step[e], s5_B_re[e], s5_B_im[e],
                                s5_C_re[e], s5_C_im[e], n_chunks)
            y_cm = s5_scan(u_cm, tables, n_chunks)
            o_b = s5_glu(y_cm, u_cm, s5_D[e], s5_w_glu[e].astype(BF16), rows=_tile(t // S5_CHUNK, 512))
            mix = matmul_cat(o_a, o_b, ev_w_out, tm=tm, tn=tn_w, layer=e)
        else:
            o = l // 2
            z = matmul(h, od_w_in, tm=tm, tn=tn_w, layer=o)
            z3 = z.reshape(bsz, seq, 6 * dh)
            o_c = conv_module(z3, conv_w[o], conv_b[o], conv_ln_g[o], conv_ln_b[o]).reshape(t, dh)
            o_d = hgrn2(z3, lb_all[l], hgrn_norm_g[o]).reshape(t, dh)
            mix = matmul_cat(o_c, o_d, od_w_out, tm=tm, tn=tn_w, layer=o)
        xf, h = resnorm(mix, xf, gains[1], gains[2])
        hidden = swiglu_gate_up(h, ffn_w_gate, ffn_w_up, tm=tm, tn=_tile(d_ff, 256), layer=l)
        down = matmul(hidden, w_down, tm=_tile(t, 512), tn=_tile(d, 512), layer=l)
        xf, h = resnorm(down, xf, gains[3], norm_gains[l + 1, 0] if l + 1 < depth else None)
    return xf.reshape(bsz, seq, d)
```

```python
import functools
import math

import jax
import jax.numpy as jnp
from jax import lax
from jax.experimental import pallas as pl
from jax.experimental.pallas import tpu as pltpu

F32 = jnp.float32
BF16 = jnp.bfloat16

A_HEAD_DIM = 64
S5_GROUP = 16
S5_STATE = 64
S5_CHUNK = 16
CONV_WIDTH = 31
CONV_HALO = 32
H_EXPAND = 128
H_CHUNK = 64
RMS_EPS = 1e-6
LN_EPS = 1e-5
LANES = 128
SUBLANES = 8
VMEM_LIMIT_BYTES = 56 * 1024 * 1024
NEG_BIG = -1e30
S5_BLOCK_GROUPS = LANES // S5_GROUP
S5_BLOCK_STATE = S5_BLOCK_GROUPS * S5_STATE


def _params(*sem):
    return pltpu.CompilerParams(dimension_semantics=sem, vmem_limit_bytes=VMEM_LIMIT_BYTES)


def _sigmoid(x):
    return 1.0 / (1.0 + jnp.exp(-x))


def _dot(a, b):
    return jnp.dot(a, b, preferred_element_type=F32)


def _dot_nt(a, b):
    return lax.dot_general(a, b, (((1,), (1,)), ((), ())), preferred_element_type=F32)


def _dot_tn(a, b):
    return lax.dot_general(a, b, (((0,), (0,)), ((), ())), preferred_element_type=F32)


def _rms(x, g):
    return x * lax.rsqrt(jnp.mean(x * x, axis=-1, keepdims=True) + RMS_EPS) * g


def _rmsnorm_kernel(x_ref, g_ref, o_ref):
    o_ref[...] = _rms(x_ref[...].astype(F32), g_ref[...]).astype(o_ref.dtype)


def rmsnorm(x, g, *, rows=512):
    t, d = x.shape
    return pl.pallas_call(
        _rmsnorm_kernel,
        out_shape=jax.ShapeDtypeStruct((t, d), BF16),
        grid=(t // rows,),
        in_specs=[pl.BlockSpec((rows, d), lambda i: (i, 0)), pl.BlockSpec((1, d), lambda i: (0, 0))],
        out_specs=pl.BlockSpec((rows, d), lambda i: (i, 0)),
        compiler_params=_params("parallel"),
        name="rmsnorm",
    )(x, g.reshape(1, d).astype(F32))


def _resnorm_kernel(y_ref, x_ref, gpost_ref, gpre_ref, xo_ref, ho_ref):
    xn = x_ref[...] + _rms(y_ref[...].astype(F32), gpost_ref[...])
    xo_ref[...] = xn
    ho_ref[...] = _rms(xn, gpre_ref[...]).astype(ho_ref.dtype)


def _resnorm_last_kernel(y_ref, x_ref, gpost_ref, xo_ref):
    xo_ref[...] = x_ref[...] + _rms(y_ref[...].astype(F32), gpost_ref[...])


def resnorm(y, x, g_post, g_pre, *, rows=256):
    t, d = x.shape
    row_spec = pl.BlockSpec((rows, d), lambda i: (i, 0))
    gain_spec = pl.BlockSpec((1, d), lambda i: (0, 0))
    if g_pre is None:
        return pl.pallas_call(
            _resnorm_last_kernel,
            out_shape=jax.ShapeDtypeStruct((t, d), F32),
            grid=(t // rows,),
            in_specs=[row_spec, row_spec, gain_spec],
            out_specs=row_spec,
            compiler_params=_params("parallel"),
            name="resnorm_last",
        )(y, x, g_post.reshape(1, d).astype(F32)), None
    return pl.pallas_call(
        _resnorm_kernel,
        out_shape=(jax.ShapeDtypeStruct((t, d), F32), jax.ShapeDtypeStruct((t, d), BF16)),
        grid=(t // rows,),
        in_specs=[row_spec, row_spec, gain_spec, gain_spec],
        out_specs=(row_spec, row_spec),
        compiler_params=_params("parallel"),
        name="resnorm",
    )(y, x, g_post.reshape(1, d).astype(F32), g_pre.reshape(1, d).astype(F32))


def _mm_kernel(x_ref, w_ref, o_ref):
    o_ref[...] = _dot(x_ref[...], w_ref[...].astype(BF16)).astype(o_ref.dtype)


def _weight_spec(w, k, tn, layer, col_start, row_block=0):
    if w.ndim == 2:
        return pl.BlockSpec((k, tn), lambda i, j: (row_block, j + col_start))
    return pl.BlockSpec((None, k, tn), lambda i, j: (layer, row_block, j + col_start))


def matmul(x, w, *, tm, tn, out_dtype=BF16, layer=0, col_start=0, n=None):
    m, k = x.shape
    n = w.shape[-1] if n is None else n
    return pl.pallas_call(
        _mm_kernel,
        out_shape=jax.ShapeDtypeStruct((m, n), out_dtype),
        grid=(m // tm, n // tn),
        in_specs=[pl.BlockSpec((tm, k), lambda i, j: (i, 0)), _weight_spec(w, k, tn, layer, col_start)],
        out_specs=pl.BlockSpec((tm, tn), lambda i, j: (i, j)),
        compiler_params=_params("parallel", "arbitrary"),
        name="matmul",
    )(x, w)


def _mm_chunk_major_kernel(x_ref, w_ref, o_ref):
    acc = _dot(x_ref[...], w_ref[...].astype(BF16))
    for p in range(o_ref.shape[0]):
        o_ref[p] = acc[:, p * LANES:(p + 1) * LANES].astype(o_ref.dtype)


def matmul_chunk_major(x, w, *, tn, n, layer=0, col_start=0):
    t, k = x.shape
    big_l = S5_CHUNK
    r = t // big_l
    return pl.pallas_call(
        _mm_chunk_major_kernel,
        out_shape=jax.ShapeDtypeStruct((big_l, n // LANES, r, LANES), BF16),
        grid=(big_l, n // tn),
        in_specs=[pl.BlockSpec((r, k), lambda l, j: (0, l)), _weight_spec(w, k, tn, layer, col_start)],
        out_specs=pl.BlockSpec((None, tn // LANES, r, LANES), lambda l, j: (l, j, 0, 0)),
        compiler_params=_params("parallel", "arbitrary"),
        name="matmul_chunk_major",
    )(x.reshape(r, big_l * k), w)


def _mm2_kernel(x1_ref, x2_ref, w1_ref, w2_ref, o_ref):
    o_ref[...] = (_dot(x1_ref[...], w1_ref[...].astype(BF16))
                  + _dot(x2_ref[...], w2_ref[...].astype(BF16))).astype(o_ref.dtype)


def matmul_cat(x1, x2, w, *, tm, tn, out_dtype=BF16, layer=0):
    m, kh = x1.shape
    n = w.shape[-1]
    return pl.pallas_call(
        _mm2_kernel,
        out_shape=jax.ShapeDtypeStruct((m, n), out_dtype),
        grid=(m // tm, n // tn),
        in_specs=[pl.BlockSpec((tm, kh), lambda i, j: (i, 0)), pl.BlockSpec((tm, kh), lambda i, j: (i, 0)),
                  _weight_spec(w, kh, tn, layer, 0, row_block=0), _weight_spec(w, kh, tn, layer, 0, row_block=1)],
        out_specs=pl.BlockSpec((tm, tn), lambda i, j: (i, j)),
        compiler_params=_params("parallel", "arbitrary"),
        name="matmul_cat",
    )(x1, x2, w, w)


def _gateup_kernel(x_ref, wg_ref, wu_ref, o_ref):
    x = x_ref[...]
    g = _dot(x, wg_ref[...].astype(BF16))
    u = _dot(x, wu_ref[...].astype(BF16))
    o_ref[...] = (g * _sigmoid(g) * u).astype(o_ref.dtype)


def swiglu_gate_up(x, wg, wu, *, tm, tn, layer=0):
    m, k = x.shape
    n = wg.shape[-1]
    w_spec = _weight_spec(wg, k, tn, layer, 0)
    return pl.pallas_call(
        _gateup_kernel,
        out_shape=jax.ShapeDtypeStruct((m, n), BF16),
        grid=(m // tm, n // tn),
        in_specs=[pl.BlockSpec((tm, k), lambda i, j: (i, 0)), w_spec, w_spec],
        out_specs=pl.BlockSpec((tm, tn), lambda i, j: (i, j)),
        compiler_params=_params("parallel", "arbitrary"),
        name="swiglu_gate_up",
    )(x, wg, wu)


A_HEADS_PER_STEP = 2


def _diffattn_kernel(lam_ref, q_ref, k_ref, v_ref, g_ref, o_ref, q2_s, vx_s, m_s, acc_s, s_s, *, blk, lam_init):
    heads = q2_s.shape[0]
    i = pl.program_id(2)

    def lanes(hh):
        return slice(hh * LANES, (hh + 1) * LANES)

    @pl.when(i == 0)
    def _():
        for hh in range(heads):
            vx_s[hh, :, :LANES] = v_ref[:, lanes(hh)]
            vx_s[hh, :, LANES:] = jnp.ones((vx_s.shape[1], LANES), vx_s.dtype)

    for hh in range(heads):
        q = q_ref[:, lanes(hh)]
        lane = lax.broadcasted_iota(jnp.int32, q.shape, 1)
        qs = q * (A_HEAD_DIM ** -0.5)
        zero = jnp.zeros_like(qs)
        q2_s[hh, :blk, :] = jnp.where(lane < A_HEAD_DIM, qs, zero)
        q2_s[hh, blk:, :] = jnp.where(lane >= A_HEAD_DIM, qs, zero)
    m_s[...] = jnp.full(m_s.shape, NEG_BIG, F32)
    acc_s[...] = jnp.zeros(acc_s.shape, F32)

    def scores(hh, j):
        return _dot_nt(q2_s[hh], k_ref[pl.ds(pl.multiple_of(j * blk, blk), blk), lanes(hh)])

    for hh in range(heads):
        s_s[hh, 0] = scores(hh, 0)

    def step(hh, j, s, diagonal):
        start = pl.multiple_of(j * blk, blk)
        if diagonal:
            row = lax.broadcasted_iota(jnp.int32, s.shape, 0)
            row = jnp.where(row >= blk, row - blk, row)
            col = lax.broadcasted_iota(jnp.int32, s.shape, 1)
            s = jnp.where(col <= row, s, NEG_BIG)
        m_prev = m_s[hh]
        m_new = jnp.maximum(m_prev, jnp.max(s, axis=-1, keepdims=True))
        alpha = jnp.exp(m_prev - m_new)
        p = jnp.concatenate([jnp.exp(s[:, c:c + LANES] - m_new) for c in range(0, blk, LANES)], axis=1)
        pv = _dot(p.astype(BF16), vx_s[hh, pl.ds(start, blk), :])
        acc_s[hh] = jnp.concatenate([alpha, alpha], axis=1) * acc_s[hh] + pv
        m_s[hh] = m_new

    def pair(jj, carry):
        j = 2 * jj
        for hh in range(heads):
            s_s[hh, 1] = scores(hh, j + 1)
            step(hh, j, s_s[hh, 0], False)
        for hh in range(heads):
            s_s[hh, 0] = scores(hh, j + 2)
            step(hh, j + 1, s_s[hh, 1], False)
        return carry

    lax.fori_loop(0, i // 2, pair, 0)

    @pl.when(i % 2 == 1)
    def _():
        for hh in range(heads):
            s_s[hh, 1] = scores(hh, i)
            step(hh, i - 1, s_s[hh, 0], False)

    for hh in range(heads):
        step(hh, i, s_s[hh, i % 2], True)

    for hh in range(heads):
        acc = acc_s[hh]
        o = acc[:blk, :LANES] / acc[:blk, LANES:] - lam_ref[0] * (acc[blk:, :LANES] / acc[blk:, LANES:])
        o_ref[:, lanes(hh)] = (_rms(o, g_ref[...]) * (1.0 - lam_init)).astype(o_ref.dtype)


def diff_attention(z3, lam, subln_g, lam_init, *, blk):
    b, s, n = z3.shape
    heads = n // 3 // LANES
    hb = min(A_HEADS_PER_STEP, heads)
    width = hb * LANES
    head_blocks = heads // hb
    kern = functools.partial(_diffattn_kernel, blk=blk, lam_init=lam_init)
    return pl.pallas_call(
        kern,
        out_shape=jax.ShapeDtypeStruct((b, s, heads * LANES), BF16),
        grid=(b, head_blocks, s // blk),
        in_specs=[
            pl.BlockSpec(memory_space=pltpu.SMEM),
            pl.BlockSpec((None, blk, width), lambda bi, h, i: (bi, i, h)),
            pl.BlockSpec((None, s, width), lambda bi, h, i: (bi, 0, head_blocks + h)),
            pl.BlockSpec((None, s, width), lambda bi, h, i: (bi, 0, 2 * head_blocks + h)),
            pl.BlockSpec((1, LANES), lambda bi, h, i: (0, 0)),
        ],
        out_specs=pl.BlockSpec((None, blk, width), lambda bi, h, i: (bi, i, h)),
        scratch_shapes=[pltpu.VMEM((hb, 2 * blk, LANES), BF16), pltpu.VMEM((hb, s, 2 * LANES), BF16),
                        pltpu.VMEM((hb, 2 * blk, LANES), F32), pltpu.VMEM((hb, 2 * blk, 2 * LANES), F32),
                        pltpu.VMEM((hb, 2, 2 * blk, blk), F32)],
        compiler_params=_params("arbitrary", "arbitrary", "arbitrary"),
        name="diff_attention",
    )(lam.reshape(1).astype(F32), z3, z3, z3, subln_g.reshape(1, LANES).astype(F32))


def _s5_tables(lam_re, lam_im, log_step, b_re, b_im, c_re, c_im, n_chunks):
    hp = lax.Precision.HIGHEST
    g_cnt, n = lam_re.shape
    ch = b_re.shape[-1]
    big_l = S5_CHUNK
    gb = S5_BLOCK_GROUPS
    blocks = g_cnt // gb
    dt = jnp.exp(log_step.astype(F32))[:, None]
    lr, li = lam_re.astype(F32), lam_im.astype(F32)
    mag = jnp.exp(lr * dt)
    ang = li * dt
    a_re, a_im = mag * jnp.cos(ang), mag * jnp.sin(ang)
    den = lr * lr + li * li
    num_re, num_im = a_re - 1.0, a_im
    fz_re = (num_re * lr + num_im * li) / den
    fz_im = (num_im * lr - num_re * li) / den
    br, bi = b_re.astype(F32), b_im.astype(F32)
    bb_re = fz_re[..., None] * br - fz_im[..., None] * bi
    bb_im = fz_re[..., None] * bi + fz_im[..., None] * br

    pr, pi = [jnp.ones_like(a_re)], [jnp.zeros_like(a_im)]
    for _ in range(big_l):
        pr.append(pr[-1] * a_re - pi[-1] * a_im)
        pi.append(pr[-2] * a_im + pi[-1] * a_re)
    pw_re, pw_im = jnp.stack(pr), jnp.stack(pi)

    cr, ci = c_re.astype(F32), c_im.astype(F32)
    bbt_re, bbt_im = bb_re.transpose(0, 2, 1), bb_im.transpose(0, 2, 1)

    p_re = pw_re[:big_l, :, None, :] * bbt_re[None] - pw_im[:big_l, :, None, :] * bbt_im[None]
    p_im = pw_re[:big_l, :, None, :] * bbt_im[None] + pw_im[:big_l, :, None, :] * bbt_re[None]
    k_lag = (jnp.einsum('gcn,jgdn->jgdc', cr, p_re, precision=hp)
             - jnp.einsum('gcn,jgdn->jgdc', ci, p_im, precision=hp))
    k_bd = (k_lag.reshape(big_l, blocks, gb, ch, 1, ch) * jnp.eye(gb, dtype=F32)[:, None, :, None])
    k_bd = k_bd.reshape(big_l, blocks, gb * ch, gb * ch)
    lag_tab = jnp.concatenate([k_bd[::-1].transpose(1, 0, 2, 3).reshape(blocks, big_l * gb * ch, gb * ch),
                               jnp.zeros((blocks, gb * ch, gb * ch), F32)], axis=1)

    n2 = 2 * n
    tile_n = jnp.tile(jnp.eye(n, dtype=F32), (1, 2))
    spread_g = jnp.repeat(jnp.eye(gb, dtype=F32), ch, axis=1)
    keep_in = (jnp.arange(gb)[:, None] % 2 == (jnp.arange(n2) // n)[None, :]).astype(F32)
    keep_out = (jnp.arange(2)[:, None] == ((jnp.arange(gb * ch) // ch) % 2)[None, :]).astype(F32)

    def tile2(x):
        return jnp.dot(x, tile_n, precision=hp)

    pwr = tile2(pw_re[:big_l][::-1]).reshape(big_l, blocks, gb, 1, n2)
    pwi = tile2(pw_im[:big_l][::-1]).reshape(big_l, blocks, gb, 1, n2)
    bbr = tile2(bbt_re).reshape(1, blocks, gb, ch, n2) * keep_in[:, None, :]
    bbi = tile2(bbt_im).reshape(1, blocks, gb, ch, n2) * keep_in[:, None, :]

    def rows_in(w):
        return w.transpose(1, 0, 2, 3, 4).reshape(blocks, big_l * gb * ch, n2)

    w_in = jnp.concatenate([rows_in(pwr * bbr - pwi * bbi), rows_in(pwr * bbi + pwi * bbr)], axis=2)

    def spread_pln(x):
        return jnp.dot(x.reshape(big_l, blocks, gb, n).transpose(1, 0, 3, 2), spread_g, precision=hp)

    def c_pn(x):
        return x.reshape(blocks, gb, ch, n).transpose(0, 3, 1, 2).reshape(blocks, 1, n, gb * ch)

    art, ait = spread_pln(pw_re[1:]), spread_pln(pw_im[1:])
    crp, cip = c_pn(cr), c_pn(ci)

    def rows_out(w):
        return (w[:, :, None] * keep_out[:, None, :]).reshape(blocks, big_l, n2, gb * ch)

    w_out = jnp.concatenate([rows_out(crp * art - cip * ait), rows_out(-(crp * ait + cip * art))], axis=2)

    sr, si = [pw_re[big_l]], [pw_im[big_l]]
    steps = max(1, int(math.ceil(math.log2(n_chunks))))
    for _ in range(steps - 1):
        sr.append(sr[-1] * sr[-1] - si[-1] * si[-1])
        si.append(2.0 * sr[-2] * si[-1])
    scan_pw = jnp.stack(sr + si, axis=1).reshape(blocks, gb, 2 * steps, n)
    scan_pw = scan_pw.transpose(0, 2, 1, 3).reshape(blocks, 2 * steps, gb * n)
    return lag_tab.astype(BF16), w_in.astype(BF16), w_out.astype(BF16), scan_pw, steps


def _s5_kernel(u_ref, lag_ref, win_ref, wout_ref, pw_ref, y_ref, win_s, wout_s, *, steps):
    big_l = u_ref.shape[0]
    pairs = S5_BLOCK_GROUPS // 2
    pair_width = 2 * S5_GROUP

    @pl.when(pl.program_id(1) == 0)
    def _():
        row_pair = (lax.broadcasted_iota(jnp.int32, (win_ref.shape[0], LANES), 0) // pair_width) % pairs
        col_pair = (lax.broadcasted_iota(jnp.int32, (LANES, LANES), 1) // pair_width) % pairs
        for k in range(pairs):
            keep_rows = jnp.where(row_pair == k, 1.0, 0.0).astype(BF16)
            keep_cols = jnp.where(col_pair == k, 1.0, 0.0).astype(BF16)
            for half in range(2):
                src = slice(half * LANES, (half + 1) * LANES)
                dst = slice((half * pairs + k) * LANES, (half * pairs + k + 1) * LANES)
                win_s[:, dst] = win_ref[:, src] * keep_rows
                for l in range(big_l):
                    wout_s[dst, l * LANES:(l + 1) * LANES] = wout_ref[l, src, :] * keep_cols

    ucat = jnp.concatenate([u_ref[l] for l in range(big_l)], axis=1)
    inj = _dot(ucat, win_s[...])
    nb = inj.shape[1] // 2
    xr, xi = inj[:, :nb], inj[:, nb:]
    chunk = lax.broadcasted_iota(jnp.int32, xr.shape, 0)
    for t in range(steps):
        d = 1 << t
        ar = pw_ref[t:t + 1, :]
        ai = pw_ref[steps + t:steps + t + 1, :]
        sr = pltpu.roll(xr, d, 0)
        si = pltpu.roll(xi, d, 0)
        ok = chunk >= d
        xr, xi = (xr + jnp.where(ok, ar * sr - ai * si, 0.0),
                  xi + jnp.where(ok, ar * si + ai * sr, 0.0))
    first = chunk == 0
    xsr = jnp.where(first, 0.0, pltpu.roll(xr, 1, 0))
    xsi = jnp.where(first, 0.0, pltpu.roll(xi, 1, 0))
    y_state = _dot(jnp.concatenate([xsr, xsi], axis=1).astype(BF16), wout_s[...])
    for m in range(big_l // 2):
        even = lag_ref[(big_l - 1 - 2 * m) * LANES:(big_l + 1) * LANES, :]
        odd = lag_ref[(big_l - 2 - 2 * m) * LANES:big_l * LANES, :]
        w = jnp.concatenate([even, odd], axis=1)
        y2 = _dot(ucat[:, :(2 * m + 2) * LANES], w) + y_state[:, 2 * m * LANES:(2 * m + 2) * LANES]
        y_ref[2 * m] = y2[:, :LANES].astype(y_ref.dtype)
        y_ref[2 * m + 1] = y2[:, LANES:].astype(y_ref.dtype)


def s5_scan(u_cm, tables, n_chunks):
    lag_tab, w_in, w_out, scan_pw, steps = tables
    big_l, blocks, rows, _ = u_cm.shape
    nb2 = 2 * S5_BLOCK_STATE
    kern = functools.partial(_s5_kernel, steps=steps)
    plane = pl.BlockSpec((big_l, None, n_chunks, LANES), lambda p, b: (0, p, b, 0))
    return pl.pallas_call(
        kern,
        out_shape=jax.ShapeDtypeStruct(u_cm.shape, BF16),
        grid=(blocks, rows // n_chunks),
        in_specs=[
            plane,
            pl.BlockSpec((None, (big_l + 1) * LANES, LANES), lambda p, b: (p, 0, 0)),
            pl.BlockSpec((None, big_l * LANES, 2 * LANES), lambda p, b: (p, 0, 0)),
            pl.BlockSpec((None, big_l, 2 * LANES, LANES), lambda p, b: (p, 0, 0, 0)),
            pl.BlockSpec((None, 2 * steps, S5_BLOCK_STATE), lambda p, b: (p, 0, 0)),
        ],
        out_specs=plane,
        scratch_shapes=[pltpu.VMEM((big_l * LANES, nb2), BF16), pltpu.VMEM((nb2, big_l * LANES), BF16)],
        compiler_params=_params("parallel", "arbitrary"),
        name="s5_scan",
    )(u_cm, lag_tab, w_in, w_out, scan_pw)


def _s5_glu_kernel(y_ref, u_ref, d_ref, w_ref, o_ref):
    blocks = y_ref.shape[0]
    y = jnp.concatenate([y_ref[p] for p in range(blocks)], axis=1).astype(F32)
    u = jnp.concatenate([u_ref[p] for p in range(blocks)], axis=1).astype(F32)
    y = y + d_ref[...] * u
    z = 0.5 * y * (1.0 + jnp.tanh(math.sqrt(2.0 / math.pi) * (y + 0.044715 * (y * y * y))))
    gate = _dot(z.astype(BF16), w_ref[...])
    o_ref[...] = (z * _sigmoid(gate)).astype(o_ref.dtype)


def s5_glu(y_cm, u_cm, d_skip, w_glu, *, rows=512):
    big_l, blocks, r, _ = y_cm.shape
    dh = blocks * LANES
    plane = pl.BlockSpec((None, blocks, rows, LANES), lambda l, i: (l, 0, i, 0))
    out = pl.pallas_call(
        _s5_glu_kernel,
        out_shape=jax.ShapeDtypeStruct((r, big_l * dh), BF16),
        grid=(big_l, r // rows),
        in_specs=[plane, plane, pl.BlockSpec((1, dh), lambda l, i: (0, 0)),
                  pl.BlockSpec((dh, dh), lambda l, i: (0, 0))],
        out_specs=pl.BlockSpec((rows, dh), lambda l, i: (i, l)),
        compiler_params=_params("parallel", "parallel"),
        name="s5_glu",
    )(y_cm, u_cm, d_skip.reshape(1, dh).astype(F32), w_glu)
    return out.reshape(r * big_l, dh)


CONV_ROWS = 64
CONV_COLS = 256


def _conv_kernel(a_ref, g_ref, w_ref, b_ref, lng_ref, lnb_ref, o_ref, h_s, sh_s, y_s, *, rows):
    i = pl.program_id(1)

    @pl.when(i == 0)
    def _():
        h_s[0:CONV_HALO, :] = jnp.zeros((CONV_HALO, h_s.shape[1]), F32)

    @pl.when(i > 0)
    def _():
        h_s[0:CONV_HALO, :] = h_s[rows:rows + CONV_HALO, :]

    h_s[CONV_HALO:CONV_HALO + rows, :] = a_ref[...].astype(F32) * _sigmoid(g_ref[...].astype(F32))
    first_tap = CONV_HALO - (CONV_WIDTH - 1)
    shifted_rows = sh_s.shape[1]

    def col_block(cb, carry):
        c0 = pl.multiple_of(cb * CONV_COLS, CONV_COLS)
        cols = pl.ds(c0, CONV_COLS)
        for k in range(1, SUBLANES):
            sh_s[k - 1, :, cols] = h_s[k:k + shifted_rows, cols]
        for r0 in range(0, rows, CONV_ROWS):
            acc = jnp.zeros((CONV_ROWS, CONV_COLS), F32)
            for j in range(CONV_WIDTH):
                k = (first_tap + j) % SUBLANES
                base = r0 + first_tap + j - k
                window = (h_s[base:base + CONV_ROWS, cols] if k == 0
                          else sh_s[k - 1, base:base + CONV_ROWS, cols])
                acc = acc + w_ref[j:j + 1, cols] * window
            y_s[r0:r0 + CONV_ROWS, cols] = acc + b_ref[:, cols]
        return carry

    lax.fori_loop(0, h_s.shape[1] // CONV_COLS, col_block, 0)
    y = y_s[...]
    mu = jnp.mean(y, axis=-1, keepdims=True)
    yc = y - mu
    var = jnp.mean(yc * yc, axis=-1, keepdims=True)
    yn = yc * lax.rsqrt(var + LN_EPS) * lng_ref[...] + lnb_ref[...]
    o_ref[...] = (yn * _sigmoid(yn)).astype(o_ref.dtype)


def conv_module(z3, w, bias, ln_g, ln_b, *, rows=256):
    b, s, n = z3.shape
    dh = n // 6
    kern = functools.partial(_conv_kernel, rows=rows)
    vec = lambda v: v.reshape(1, dh).astype(F32)
    vec_spec = pl.BlockSpec((1, dh), lambda bi, i: (0, 0))
    return pl.pallas_call(
        kern,
        out_shape=jax.ShapeDtypeStruct((b, s, dh), BF16),
        grid=(b, s // rows),
        in_specs=[pl.BlockSpec((None, rows, dh), lambda bi, i: (bi, i, 0)),
                  pl.BlockSpec((None, rows, dh), lambda bi, i: (bi, i, 1)),
                  pl.BlockSpec((CONV_WIDTH, dh), lambda bi, i: (0, 0)), vec_spec, vec_spec, vec_spec],
        out_specs=pl.BlockSpec((None, rows, dh), lambda bi, i: (bi, i, 0)),
        scratch_shapes=[pltpu.VMEM((CONV_HALO + rows, dh), F32),
                        pltpu.VMEM((SUBLANES - 1, CONV_HALO + rows - SUBLANES, dh), F32),
                        pltpu.VMEM((rows, dh), F32)],
        compiler_params=_params("arbitrary", "arbitrary"),
        name="conv_module",
    )(z3, z3, w.astype(F32), vec(bias), vec(ln_g), vec(ln_b))


H_HEADS_PER_STEP = 16


def _split3(x):
    hi = x.astype(BF16)
    r1 = x - hi.astype(F32)
    mid = r1.astype(BF16)
    lo = (r1 - mid.astype(F32)).astype(BF16)
    return hi, mid, lo


def _hgrn_kernel(q_ref, f_ref, i_ref, g_ref, lb_ref, ng_ref, o_ref, st_s, *, rows):
    @pl.when(pl.program_id(2) == 0)
    def _():
        st_s[...] = jnp.zeros(st_s.shape, F32)

    big_l = H_CHUNK
    dk = H_EXPAND
    n_chunks = rows // big_l
    r_i = lax.broadcasted_iota(jnp.int32, (rows, rows), 0)
    c_i = lax.broadcasted_iota(jnp.int32, (rows, rows), 1)
    causal = jnp.logical_and(c_i <= r_i, r_i - c_i <= r_i % big_l)
    tri = jnp.where(causal, 1.0, 0.0).astype(BF16)

    def per_chunk_rows(x, row):
        return jnp.concatenate([jnp.broadcast_to(x[c * big_l + row:c * big_l + row + 1, :], (big_l, dk))
                                for c in range(n_chunks)], axis=0)

    for hh in range(st_s.shape[0]):
        cs = slice(hh * dk, (hh + 1) * dk)
        lb = lb_ref[:, cs]
        q = q_ref[:, cs].astype(F32)
        q = q * _sigmoid(q)
        fg = lb + (1.0 - lb) * _sigmoid(f_ref[:, cs].astype(F32))
        log_f = jnp.log(fg)
        k = 1.0 - fg
        v = i_ref[:, cs]
        cum3 = _dot(tri, jnp.concatenate(_split3(log_f), axis=1))
        bcum = cum3[:, :dk] + cum3[:, dk:2 * dk] + cum3[:, 2 * dk:]
        b_mid = per_chunk_rows(bcum, big_l // 2 - 1)
        b_last = per_chunk_rows(bcum, big_l - 1)
        q_in = (q * jnp.exp(bcum - b_mid)).astype(BF16)
        k_in = (k * jnp.exp(b_mid - bcum)).astype(BF16)
        scores = jnp.where(causal, _dot_nt(q_in, k_in), 0.0)
        o_intra = _dot(scores.astype(BF16), v)
        q_dec = (q * jnp.exp(bcum)).astype(BF16)
        k_dec = (k * jnp.exp(b_last - bcum)).astype(BF16)
        st = st_s[hh]
        o_inter = []
        for c in range(n_chunks):
            sl = slice(c * big_l, (c + 1) * big_l)
            o_inter.append(_dot_nt(q_dec[sl], st.astype(BF16)))
            st = st * jnp.exp(bcum[(c + 1) * big_l - 1:(c + 1) * big_l, :]) + _dot_tn(v[sl], k_dec[sl])
        st_s[hh] = st
        o = o_intra + jnp.concatenate(o_inter, axis=0)
        gate = g_ref[:, cs].astype(F32)
        o_ref[:, cs] = (_rms(o, ng_ref[...]) * (gate * _sigmoid(gate))).astype(o_ref.dtype)


def hgrn2(z3, lb, norm_g, *, rows=256):
    b, s, n = z3.shape
    dh = n // 6
    hb = min(H_HEADS_PER_STEP, dh // H_EXPAND)
    width = hb * H_EXPAND
    head_blocks = dh // width
    kern = functools.partial(_hgrn_kernel, rows=rows)

    def col(group):
        return pl.BlockSpec((None, rows, width), lambda bi, h, i: (bi, i, group * head_blocks + h))

    return pl.pallas_call(
        kern,
        out_shape=jax.ShapeDtypeStruct((b, s, dh), BF16),
        grid=(b, head_blocks, s // rows),
        in_specs=[col(2), col(3), col(4), col(5),
                  pl.BlockSpec((1, width), lambda bi, h, i: (0, h)),
                  pl.BlockSpec((1, H_EXPAND), lambda bi, h, i: (0, 0))],
        out_specs=pl.BlockSpec((None, rows, width), lambda bi, h, i: (bi, i, h)),
        scratch_shapes=[pltpu.VMEM((hb, H_EXPAND, H_EXPAND), F32)],
        compiler_params=_params("parallel", "parallel", "arbitrary"),
        name="hgrn2",
    )(z3, z3, z3, z3, lb.reshape(1, dh).astype(F32), norm_g.reshape(1, H_EXPAND).astype(F32))


def _tile(n, want):
    if n <= want:
        return n
    t = want - want % LANES
    while n % t:
        t -= LANES
    return t


def kernel(x, norm_gains, ffn_w_gate, ffn_w_up, ffn_w_down, ev_w_in, ev_w_out, diff_lambda, diff_subln, s5_lambda_re, s5_lambda_im, s5_log_step, s5_B_re, s5_B_im, s5_C_re, s5_C_im, s5_D, s5_w_glu, od_w_in, od_w_out, conv_w, conv_b, conv_ln_g, conv_ln_b, hgrn_lb, hgrn_norm_g):
    bsz, seq, d = x.shape
    depth = norm_gains.shape[0]
    t = bsz * seq
    dh = d // 2
    d_ff = ffn_w_gate.shape[-1]
    tm = _tile(t, 1024)
    attn_blk = _tile(seq, 512)
    n_chunks = seq // S5_CHUNK
    tn_w = _tile(dh, 512)

    lb_all = jnp.cumsum(jax.nn.softmax(hgrn_lb.astype(F32), axis=0), axis=0)
    lb_all = lb_all - lb_all[0]

    w_down = ffn_w_down.astype(BF16)

    xf = x.reshape(t, d)
    h = rmsnorm(xf, norm_gains[0, 0])
    for l in range(depth):
        gains = norm_gains[l]
        if l % 2 == 0:
            e = l // 2
            lam_init = 0.8 - 0.6 * math.exp(-0.3 * l)
            qkv = matmul(h, ev_w_in, tm=tm, tn=tn_w, layer=e, n=3 * dh)
            u_cm = matmul_chunk_major(h, ev_w_in, tn=tn_w, n=dh, layer=e, col_start=3 * dh // tn_w)
            lf = diff_lambda[e].astype(F32)
            lam = jnp.exp(jnp.sum(lf[0] * lf[1])) - jnp.exp(jnp.sum(lf[2] * lf[3])) + lam_init
            o_a = diff_attention(qkv.reshape(bsz, seq, 3 * dh), lam, diff_subln[e], lam_init,
                                 blk=attn_blk).reshape(t, dh)
            tables = _s5_tables(s5_lambda_re[e], s5_lambda_im[e], s5_log_step[e], s5_B_re[e], s5_B_im[e],
                                s5_C_re[e], s5_C_im[e], n_chunks)
            y_cm = s5_scan(u_cm, tables, n_chunks)
            o_b = s5_glu(y_cm, u_cm, s5_D[e], s5_w_glu[e].astype(BF16), rows=_tile(t // S5_CHUNK, 512))
            mix = matmul_cat(o_a, o_b, ev_w_out, tm=tm, tn=tn_w, layer=e)
        else:
            o = l // 2
            z = matmul(h, od_w_in, tm=tm, tn=tn_w, layer=o)
            z3 = z.reshape(bsz, seq, 6 * dh)
            o_c = conv_module(z3, conv_w[o], conv_b[o], conv_ln_g[o], conv_ln_b[o]).reshape(t, dh)
            o_d = hgrn2(z3, lb_all[l], hgrn_norm_g[o]).reshape(t, dh)
            mix = matmul_cat(o_c, o_d, od_w_out, tm=tm, tn=tn_w, layer=o)
        xf, h = resnorm(mix, xf, gains[1], gains[2])
        hidden = swiglu_gate_up(h, ffn_w_gate, ffn_w_up, tm=tm, tn=_tile(d_ff, 256), layer=l)
        down = matmul(hidden, w_down, tm=_tile(t, 512), tn=_tile(d, 512), layer=l)
        xf, h = resnorm(down, xf, gains[3], norm_gains[l + 1, 0] if l + 1 < depth else None)
    return xf.reshape(bsz, seq, d)
```
